```python
import math
import jax, jax.numpy as jnp
from jax import lax
import numpy as np

D_MODEL = 1024
BATCH = 2
SEQ = 8192
DEPTH = 2
DEC_BATCH = 128
DEC_SEQ = 4
PAST_LEN = 8192
PAGE_SIZE = 128

N_BRANCH = 4
BRANCH_W = D_MODEL // 2
GLA_HEADS = 4
GLA_DV = BRANCH_W // GLA_HEADS
GLA_DK = GLA_DV // 2
GLA_GATE_RANK = 16
GLA_TAU = 16.0
GLA_CHUNK = 64
LRU_W = BRANCH_W
LRU_BLOCKS = 8
LRU_BW = LRU_W // LRU_BLOCKS
LRU_C = 8.0
CONV_W = 4
MLA_HEADS = 4
MLA_NOPE = 64
MLA_ROPE = 32
MLA_V = BRANCH_W // MLA_HEADS
MLA_Q_LORA = D_MODEL // 4
MLA_KV_LORA = D_MODEL // 4
MLA_ROW = MLA_KV_LORA + MLA_ROPE
MLA_SCALE = (MLA_NOPE + MLA_ROPE) ** -0.5
ROPE_THETA = 10000.0
DSA_HEADS = 4
DSA_KV_HEADS = 1
DSA_HD = BRANCH_W // DSA_HEADS
DSA_IDX_HEADS = 4
DSA_IDX_DIM = 64
DSA_TOPK = 256
Q_BLOCK = 128
EPS = 1e-6
IN_SIZES = (
    GLA_HEADS * GLA_DK, GLA_HEADS * GLA_DK, GLA_HEADS * GLA_DV, GLA_GATE_RANK, BRANCH_W,
    LRU_W, LRU_W,
    MLA_Q_LORA, MLA_KV_LORA, MLA_ROPE, BRANCH_W,
    DSA_HEADS * DSA_HD, DSA_KV_HEADS * DSA_HD, DSA_KV_HEADS * DSA_HD,
    DSA_IDX_HEADS * DSA_IDX_DIM, DSA_IDX_DIM, DSA_IDX_HEADS, BRANCH_W,
    N_BRANCH * D_MODEL,
)
D_IN = sum(IN_SIZES)

kernel_name = 'hybrid_gla_lru_mla_dsa_step'


def rmsnorm(x, g):
    xf = x.astype(jnp.float32)
    y = xf * lax.rsqrt(jnp.mean(xf * xf, axis=-1, keepdims=True) + EPS)
    return (y * g.astype(jnp.float32)).astype(x.dtype)


def split_in(p):
    offs = np.cumsum(IN_SIZES)[:-1].tolist()
    return jnp.split(p, offs, axis=-1)


def rope(x, pos):
    half = x.shape[-1] // 2
    inv = ROPE_THETA ** (-jnp.arange(half, dtype=jnp.float32) / half)
    ang = pos.astype(jnp.float32)[:, None] * inv[None, :]
    shape = (pos.shape[0],) + (1,) * (x.ndim - 3) + (half,)
    cos = jnp.cos(ang).reshape(shape)
    sin = jnp.sin(ang).reshape(shape)
    xf = x.astype(jnp.float32)
    x1, x2 = xf[..., :half], xf[..., half:]
    return jnp.concatenate([x1 * cos - x2 * sin, x2 * cos + x1 * sin], axis=-1).astype(x.dtype)


def over_query_blocks(fn, qs, q_pos):
    B, T = qs[0].shape[:2]
    nb = T // Q_BLOCK
    def to_blocks(t):
        return jnp.moveaxis(t.reshape((B, nb, Q_BLOCK) + t.shape[2:]), 1, 0)
    out = lax.map(lambda a: fn(*a), tuple(to_blocks(t) for t in qs) + (q_pos.reshape(nb, Q_BLOCK),))
    return jnp.moveaxis(out, 0, 1).reshape((B, T) + out.shape[3:])


def gla_mix(q, k, v, g, s0):
    B, L, H, DK = q.shape
    DV = v.shape[-1]
    C = math.gcd(GLA_CHUNK, L)
    n = L // C
    f32 = jnp.float32
    q, k, g = (t.astype(f32).reshape(B, n, C, H, DK) for t in (q, k, g))
    v = v.astype(f32).reshape(B, n, C, H, DV)
    b = jnp.cumsum(g, axis=2)
    b_last = b[:, :, -1]
    q_dec = q * jnp.exp(b)
    k_dec = k * jnp.exp(-b)
    mask = jnp.tril(jnp.ones((C, C), dtype=bool))
    att = jnp.where(mask, jnp.einsum('bnthd,bnshd->bnhts', q_dec, k_dec), 0.0)
    o_intra = jnp.einsum('bnhts,bnshe->bnthe', att, v)
    k_tail = k * jnp.exp(b_last[:, :, None] - b)
    chunk_kv = jnp.einsum('bnshd,bnshe->bnhde', k_tail, v)
    decay = jnp.exp(b_last)

    def step(S, inp):
        dec, kv = inp
        return dec[..., None] * S + kv, S

    s_fin, s_prev = lax.scan(step, s0.astype(f32),
                             (jnp.moveaxis(decay, 1, 0), jnp.moveaxis(chunk_kv, 1, 0)))
    s_prev = jnp.moveaxis(s_prev, 0, 1)
    o_inter = jnp.einsum('bnthd,bnhde->bnthe', q_dec, s_prev)
    return (o_intra + o_inter).reshape(B, L, H, DV), s_fin


def gla_branch(gq, gk, gv, glr, ggate, w_g2, b_g, norm_g, s0):
    B, L, _ = gq.shape
    f32 = jnp.float32
    q = gq.reshape(B, L, GLA_HEADS, GLA_DK) * GLA_DK ** -0.5
    k = gk.reshape(B, L, GLA_HEADS, GLA_DK)
    v = gv.reshape(B, L, GLA_HEADS, GLA_DV)
    g = jax.nn.log_sigmoid((glr @ w_g2 + b_g).astype(f32)).reshape(B, L, GLA_HEADS, GLA_DK) / GLA_TAU
    o, s = gla_mix(q, k, v, g, s0)
    o = rmsnorm(o, norm_g).reshape(B, L, BRANCH_W) * jax.nn.silu(ggate.astype(f32))
    return o.astype(gq.dtype), s.astype(s0.dtype)


def causal_conv(x, buf, w, b):
    L = x.shape[1]
    xp = jnp.concatenate([buf.astype(x.dtype), x], axis=1)
    y = b + sum(xp[:, j:j + L] * w[j] for j in range(CONV_W))
    return y, xp[:, -(CONV_W - 1):]


def rglru(x, w_r, b_r, w_i, b_i, lam, h0):
    B, L, W = x.shape
    f32 = jnp.float32
    xf = x.astype(f32)
    xb = xf.reshape(B, L, LRU_BLOCKS, LRU_BW)
    r = jax.nn.sigmoid(jnp.einsum('blnd,nde->blne', xb, w_r.astype(f32)).reshape(B, L, W) + b_r.astype(f32))
    i = jax.nn.sigmoid(jnp.einsum('blnd,nde->blne', xb, w_i.astype(f32)).reshape(B, L, W) + b_i.astype(f32))
    log_a = -LRU_C * r * jax.nn.softplus(-lam.astype(f32))
    a = jnp.exp(log_a)
    u = jnp.sqrt(-jnp.expm1(2.0 * log_a)) * (i * xf)

    def step(h, au):
        a_t, u_t = au
        h = a_t * h + u_t
        return h, h

    h_fin, hs = lax.scan(step, h0.astype(f32), (jnp.swapaxes(a, 0, 1), jnp.swapaxes(u, 0, 1)))
    return jnp.swapaxes(hs, 0, 1), h_fin


def mla_project(cq, ckv, kr, q_norm, w_uq, kv_norm, w_uk, pos):
    B, L, _ = cq.shape
    q = (rmsnorm(cq, q_norm) @ w_uq).reshape(B, L, MLA_HEADS, MLA_NOPE + MLA_ROPE)
    q_nope = q[..., :MLA_NOPE]
    q_rope = rope(q[..., MLA_NOPE:], pos)
    q_lat = jnp.einsum('blhd,rhd->blhr', q_nope, w_uk)
    c = rmsnorm(ckv, kv_norm)
    k_rope = rope(kr, pos)
    return jnp.concatenate([q_lat, q_rope], axis=-1), jnp.concatenate([c, k_rope], axis=-1)


def mla_attend(q_cat, segments, q_pos):
    scores = []
    for rows, k_pos in segments:
        s = jnp.einsum('bthr,bsr->bhts', q_cat, rows).astype(jnp.float32) * MLA_SCALE
        scores.append(jnp.where(k_pos[None, :] <= q_pos[:, None], s, -jnp.inf))
    p = jax.nn.softmax(jnp.concatenate(scores, axis=-1), axis=-1)
    out = 0.0
    start = 0
    for rows, _ in segments:
        n = rows.shape[1]
        out = out + jnp.einsum('bhts,bsr->bthr', p[..., start:start + n].astype(rows.dtype), rows[..., :MLA_KV_LORA])
        start += n
    return out


def dsa_attend(q, qi, wi, segments, gather_kv, q_pos, topk):
    f32 = jnp.float32
    parts = []
    for kidx, k_pos in segments:
        sc = jax.nn.relu(jnp.einsum('bthd,bsd->bths', qi, kidx).astype(f32))
        sc = jnp.einsum('bths,bth->bts', sc, wi.astype(f32))
        parts.append(jnp.where(k_pos[None, :] <= q_pos[:, None], sc, -jnp.inf))
    score = jnp.concatenate(parts, axis=-1)
    top_val, top_idx = lax.top_k(score, topk)
    valid = top_val > -jnp.inf
    kv = gather_kv(top_idx)
    k_sel, v_sel = kv[:, :, :, 0], kv[:, :, :, 1]
    B, T, H, HD = q.shape
    qg = q.reshape(B, T, DSA_KV_HEADS, H // DSA_KV_HEADS, HD)
    s = jnp.einsum('btngd,btsnd->btngs', qg, k_sel).astype(f32) * HD ** -0.5
    s = jnp.where(valid[:, :, None, None, :], s, -jnp.inf)
    p = jax.nn.softmax(s, axis=-1)
    o = jnp.einsum('btngs,btsnd->btngd', p.astype(v_sel.dtype), v_sel)
    return o.reshape(B, T, H * HD)


def gather_paged(pool, page_table, new_rows, idx, past_len):
    DB = idx.shape[0]
    past = idx < past_len
    ip = jnp.minimum(idx, past_len - 1)
    phys = page_table[jnp.arange(DB)[:, None, None], ip // PAGE_SIZE]
    from_pool = pool[phys, ip % PAGE_SIZE]
    inew = jnp.clip(idx - past_len, 0, new_rows.shape[1] - 1)
    from_new = jax.vmap(lambda r, i: r[i])(new_rows, inew)
    cond = past.reshape(past.shape + (1,) * (from_pool.ndim - past.ndim))
    return jnp.where(cond, from_pool, from_new)


def mixer_layer(x, pos, gla_s0, conv_buf, lru_h0, attend_mla, attend_dsa, lw):
    B, L, _ = x.shape
    f32 = jnp.float32
    h = rmsnorm(x, lw['ln_gain'])
    (gq, gk, gv, glr, ggate, lx, lgate, cq, ckv, kr, cgate,
     dq, dk, dv, dqi, dki, dwi, dgate, merge) = split_in(h @ lw['w_in'])
    o_a, gla_s = gla_branch(gq, gk, gv, glr, ggate, lw['gla_w_g2'], lw['gla_b_g'], lw['gla_norm'], gla_s0)
    xc, conv_new = causal_conv(lx, conv_buf, lw['lru_conv_w'], lw['lru_conv_b'])
    hs, lru_h = rglru(xc, lw['lru_w_r'], lw['lru_b_r'], lw['lru_w_i'], lw['lru_b_i'], lw['lru_lambda'], lru_h0)
    o_b = (hs * jax.nn.silu(lgate.astype(f32))).astype(x.dtype)
    q_cat, mla_rows = mla_project(cq, ckv, kr, lw['mla_q_norm'], lw['mla_w_uq'], lw['mla_kv_norm'], lw['mla_w_uk'], pos)
    o_lat = attend_mla(q_cat, mla_rows)
    o_c = jnp.einsum('blhr,rhe->blhe', o_lat, lw['mla_w_uv']).reshape(B, L, BRANCH_W)
    o_c = (o_c.astype(f32) * jax.nn.silu(cgate.astype(f32))).astype(x.dtype)
    q = dq.reshape(B, L, DSA_HEADS, DSA_HD)
    kv_rows = jnp.stack([dk.reshape(B, L, DSA_KV_HEADS, DSA_HD), dv.reshape(B, L, DSA_KV_HEADS, DSA_HD)], axis=2)
    qi = dqi.reshape(B, L, DSA_IDX_HEADS, DSA_IDX_DIM) * DSA_IDX_DIM ** -0.5
    wi = dwi * DSA_IDX_HEADS ** -0.5
    o_d = attend_dsa(q, qi, wi, dki, kv_rows)
    o_d = (o_d.astype(f32) * jax.nn.silu(dgate.astype(f32))).astype(x.dtype)
    branches = jnp.stack([o_a, o_b, o_c, o_d], axis=2)
    proj = jnp.einsum('blnc,ncd->blnd', branches, lw['w_branch'])
    gates = jax.nn.sigmoid(merge.reshape(B, L, N_BRANCH, D_MODEL).astype(f32))
    mixed = jnp.sum(gates * proj.astype(f32), axis=2).astype(x.dtype)
    y = x + mixed @ lw['w_out']
    return y, gla_s, conv_new, lru_h.astype(lru_h0.dtype), mla_rows, kv_rows, dki


def setup_inputs(seed: int = 0) -> dict:
    key = jax.random.key(seed)
    ks = iter(jax.random.split(key, 40))
    f32 = jnp.float32

    def nrm(shape, scale):
        return jax.random.normal(next(ks), shape, f32) * scale

    def gain(shape):
        return 1.0 + nrm(shape, 0.01)

    n_pages = PAST_LEN // PAGE_SIZE
    n_pool = (DEC_BATCH * n_pages * 5) // 4
    x_prompt = nrm((BATCH, SEQ, D_MODEL), 1.0)
    x_sample = nrm((DEC_BATCH, DEC_SEQ, D_MODEL), 1.0)
    cache_mla = nrm((DEPTH, n_pool, PAGE_SIZE, MLA_ROW), 1.0)
    cache_dsa_kv = nrm((DEPTH, n_pool, PAGE_SIZE, 2, DSA_KV_HEADS, DSA_HD), 1.0)
    cache_dsa_kidx = nrm((DEPTH, n_pool, PAGE_SIZE, DSA_IDX_DIM), 1.0)
    state_gla = nrm((DEPTH, DEC_BATCH, GLA_HEADS, GLA_DK, GLA_DV), 1.0)
    state_lru_h = nrm((DEPTH, DEC_BATCH, LRU_W), 0.5)
    state_lru_conv = nrm((DEPTH, DEC_BATCH, CONV_W - 1, LRU_W), 1.0)
    page_table = jax.random.permutation(next(ks), n_pool)[: DEC_BATCH * n_pages].reshape(DEC_BATCH, n_pages).astype(jnp.int32)
    u = jax.random.uniform(next(ks), (DEPTH, LRU_W), f32, 0.9, 0.999)
    s = u ** (1.0 / LRU_C)
    lru_lambda = jnp.log(s) - jnp.log1p(-s)
    return {
        'x_prompt': x_prompt,
        'x_sample': x_sample,
        'cache_mla': cache_mla,
        'cache_dsa_kv': cache_dsa_kv,
        'cache_dsa_kidx': cache_dsa_kidx,
        'state_gla': state_gla,
        'state_lru_h': state_lru_h,
        'state_lru_conv': state_lru_conv,
        'page_table': page_table,
        'ln_gain': gain((DEPTH, D_MODEL)),
        'w_in': nrm((DEPTH, D_MODEL, D_IN), D_MODEL ** -0.5),
        'gla_w_g2': nrm((DEPTH, GLA_GATE_RANK, GLA_HEADS * GLA_DK), GLA_GATE_RANK ** -0.5),
        'gla_b_g': nrm((DEPTH, GLA_HEADS * GLA_DK), 0.1),
        'gla_norm': gain((DEPTH, GLA_DV)),
        'lru_conv_w': nrm((DEPTH, CONV_W, LRU_W), CONV_W ** -0.5),
        'lru_conv_b': nrm((DEPTH, LRU_W), 0.01),
        'lru_w_r': nrm((DEPTH, LRU_BLOCKS, LRU_BW, LRU_BW), LRU_BW ** -0.5),
        'lru_b_r': nrm((DEPTH, LRU_W), 0.01),
        'lru_w_i': nrm((DEPTH, LRU_BLOCKS, LRU_BW, LRU_BW), LRU_BW ** -0.5),
        'lru_b_i': nrm((DEPTH, LRU_W), 0.01),
        'lru_lambda': lru_lambda,
        'mla_q_norm': gain((DEPTH, MLA_Q_LORA)),
        'mla_w_uq': nrm((DEPTH, MLA_Q_LORA, MLA_HEADS * (MLA_NOPE + MLA_ROPE)), MLA_Q_LORA ** -0.5),
        'mla_kv_norm': gain((DEPTH, MLA_KV_LORA)),
        'mla_w_uk': nrm((DEPTH, MLA_KV_LORA, MLA_HEADS, MLA_NOPE), MLA_KV_LORA ** -0.5),
        'mla_w_uv': nrm((DEPTH, MLA_KV_LORA, MLA_HEADS, MLA_V), MLA_KV_LORA ** -0.5),
        'w_branch': nrm((DEPTH, N_BRANCH, BRANCH_W, D_MODEL), BRANCH_W ** -0.5),
        'w_out': nrm((DEPTH, D_MODEL, D_MODEL), 0.5 * D_MODEL ** -0.5),
        'final_gain': gain((D_MODEL,)),
    }


def reference(x_prompt, x_sample, cache_mla, cache_dsa_kv, cache_dsa_kidx, state_gla, state_lru_h,
              state_lru_conv, page_table, ln_gain, w_in, gla_w_g2, gla_b_g, gla_norm, lru_conv_w,
              lru_conv_b, lru_w_r, lru_b_r, lru_w_i, lru_b_i, lru_lambda, mla_q_norm, mla_w_uq,
              mla_kv_norm, mla_w_uk, mla_w_uv, w_branch, w_out, final_gain):
    B, S, _ = x_prompt.shape
    DB, T, _ = x_sample.shape
    n_pages = page_table.shape[1]
    past_len = n_pages * PAGE_SIZE
    pos_p = jnp.arange(S)
    pos_s = past_len + jnp.arange(T)
    pos_past = jnp.arange(past_len)
    topk_p = min(DSA_TOPK, S // 4)
    topk_s = min(DSA_TOPK, (past_len + T) // 4)
    dt = x_prompt.dtype
    gla0 = jnp.zeros((B, GLA_HEADS, GLA_DK, GLA_DV), dt)
    conv0 = jnp.zeros((B, CONV_W - 1, LRU_W), dt)
    h0 = jnp.zeros((B, LRU_W), dt)

    xp, xs = x_prompt, x_sample
    mla_p, mla_s, kv_p, kv_s, ki_p, ki_s = [], [], [], [], [], []
    gla_p, gla_s, lh_p, lh_s, lc_p, lc_s = [], [], [], [], [], []
    for l in range(DEPTH):
        lw = {
            'ln_gain': ln_gain[l], 'w_in': w_in[l], 'gla_w_g2': gla_w_g2[l], 'gla_b_g': gla_b_g[l],
            'gla_norm': gla_norm[l], 'lru_conv_w': lru_conv_w[l], 'lru_conv_b': lru_conv_b[l],
            'lru_w_r': lru_w_r[l], 'lru_b_r': lru_b_r[l], 'lru_w_i': lru_w_i[l], 'lru_b_i': lru_b_i[l],
            'lru_lambda': lru_lambda[l], 'mla_q_norm': mla_q_norm[l], 'mla_w_uq': mla_w_uq[l],
            'mla_kv_norm': mla_kv_norm[l], 'mla_w_uk': mla_w_uk[l], 'mla_w_uv': mla_w_uv[l],
            'w_branch': w_branch[l], 'w_out': w_out[l],
        }

        def mla_prompt(q_cat, rows):
            return over_query_blocks(lambda qb, pb: mla_attend(qb, ((rows, pos_p),), pb), (q_cat,), pos_p)

        def dsa_prompt(q, qi, wi, ki, kv):
            gather = lambda idx: jax.vmap(lambda r, i: r[i])(kv, idx)
            return over_query_blocks(
                lambda qb, qib, wib, pb: dsa_attend(qb, qib, wib, ((ki, pos_p),), gather, pb, topk_p),
                (q, qi, wi), pos_p)

        xp, g_s, c_b, l_h, m_r, kv_r, ki_r = mixer_layer(xp, pos_p, gla0, conv0, h0, mla_prompt, dsa_prompt, lw)
        mla_p.append(m_r); kv_p.append(kv_r); ki_p.append(ki_r)
        gla_p.append(g_s); lh_p.append(l_h); lc_p.append(c_b)

        mla_past = cache_mla[l][page_table].reshape(DB, past_len, MLA_ROW)
        kidx_past = cache_dsa_kidx[l][page_table].reshape(DB, past_len, DSA_IDX_DIM)
        kv_pool = cache_dsa_kv[l]

        def mla_sample(q_cat, rows):
            return mla_attend(q_cat, ((mla_past, pos_past), (rows, pos_s)), pos_s)

        def dsa_sample(q, qi, wi, ki, kv):
            gather = lambda idx: gather_paged(kv_pool, page_table, kv, idx, past_len)
            return dsa_attend(q, qi, wi, ((kidx_past, pos_past), (ki, pos_s)), gather, pos_s, topk_s)

        xs, g_s, c_b, l_h, m_r, kv_r, ki_r = mixer_layer(xs, pos_s, state_gla[l], state_lru_conv[l],
                                                        state_lru_h[l], mla_sample, dsa_sample, lw)
        mla_s.append(m_r); kv_s.append(kv_r); ki_s.append(ki_r)
        gla_s.append(g_s); lh_s.append(l_h); lc_s.append(c_b)

    y_prompt = rmsnorm(xp, final_gain)
    y_sample = rmsnorm(xs, final_gain)
    new_mla_prompt = jnp.stack(mla_p)
    new_mla_sample = jnp.stack(mla_s)
    new_dsa_kv_prompt = jnp.stack(kv_p)
    new_dsa_kv_sample = jnp.stack(kv_s)
    new_dsa_kidx_prompt = jnp.stack(ki_p)
    new_dsa_kidx_sample = jnp.stack(ki_s)
    new_gla_prompt = jnp.stack(gla_p)
    new_gla_sample = jnp.stack(gla_s)
    new_lru_h_prompt = jnp.stack(lh_p)
    new_lru_h_sample = jnp.stack(lh_s)
    new_lru_conv_prompt = jnp.stack(lc_p)
    new_lru_conv_sample = jnp.stack(lc_s)
    return (y_prompt, y_sample, new_mla_prompt, new_mla_sample, new_dsa_kv_prompt, new_dsa_kv_sample,
            new_dsa_kidx_prompt, new_dsa_kidx_sample, new_gla_prompt, new_gla_sample,
            new_lru_h_prompt, new_lru_h_sample, new_lru_conv_prompt, new_lru_conv_sample)
```

```python
import functools
import math

import numpy as np
import jax
import jax.numpy as jnp
from jax import lax
from jax.experimental import pallas as pl
from jax.experimental.pallas import tpu as pltpu

f32 = jnp.float32
bf16 = jnp.bfloat16
MXU_DTYPE = bf16

LANE = 128
SUB = 8
VMEM_LIMIT = 56 * 1024 * 1024

D_MODEL = 1024
PAGE_SIZE = 128
N_BRANCH = 4
BRANCH_W = D_MODEL // 2
GLA_HEADS = 4
GLA_DV = BRANCH_W // GLA_HEADS
GLA_DK = GLA_DV // 2
GLA_GATE_RANK = 16
GLA_TAU = 16.0
GLA_CHUNK = 64
LRU_W = BRANCH_W
LRU_BLOCKS = 8
LRU_BW = LRU_W // LRU_BLOCKS
LRU_C = 8.0
CONV_W = 4
MLA_HEADS = 4
MLA_NOPE = 64
MLA_ROPE = 32
MLA_V = BRANCH_W // MLA_HEADS
MLA_Q_LORA = D_MODEL // 4
MLA_KV_LORA = D_MODEL // 4
MLA_ROW = MLA_KV_LORA + MLA_ROPE
MLA_SCALE = (MLA_NOPE + MLA_ROPE) ** -0.5
MLA_KPAD = MLA_KV_LORA + LANE
ROPE_THETA = 10000.0
DSA_HEADS = 4
DSA_KV_HEADS = 1
DSA_HD = BRANCH_W // DSA_HEADS
DSA_IDX_HEADS = 4
DSA_IDX_DIM = 64
DSA_TOPK = 256
EPS = 1e-6
IN_SIZES = (
    GLA_HEADS * GLA_DK, GLA_HEADS * GLA_DK, GLA_HEADS * GLA_DV, GLA_GATE_RANK, BRANCH_W,
    LRU_W, LRU_W,
    MLA_Q_LORA, MLA_KV_LORA, MLA_ROPE, BRANCH_W,
    DSA_HEADS * DSA_HD, DSA_KV_HEADS * DSA_HD, DSA_KV_HEADS * DSA_HD,
    DSA_IDX_HEADS * DSA_IDX_DIM, DSA_IDX_DIM, DSA_IDX_HEADS, BRANCH_W,
    N_BRANCH * D_MODEL,
)
(_GQ, _GK, _GV, _GLR, _GGATE, _LX, _LGATE, _CQ, _CKV, _KR, _CGATE,
 _DQ, _DK, _DV, _DQI, _DKI, _DWI, _DGATE, _MERGE) = range(len(IN_SIZES))
_IN_OFFS = np.concatenate([[0], np.cumsum(IN_SIZES)]).tolist()

OFF_MERGE = 0
OFF_GV = 4096
OFF_GGATE = 4608
OFF_LX = 5120
OFF_LGATE = 5632
OFF_CGATE = 6144
OFF_DQ = 6656
OFF_DGATE = 7168
OFF_DQI = 7680
OFF_GQ = 8192
OFF_GK = 8448
OFF_CQ = 8704
OFF_CKV = 8960
OFF_DKV = 9216
OFF_GLR = 9472
OFF_KR = 9600
OFF_DKI = 9728
W_PAD = 10240

KEY_NEG_INF = -2139095041
INT_MIN = -2147483648

NT_DIMS = (((1,), (1,)), ((), ()))
TN_DIMS = (((0,), (0,)), ((), ()))


def _cparams(*sem):
    return pltpu.CompilerParams(dimension_semantics=sem, vmem_limit_bytes=VMEM_LIMIT)


def _mm(a, b):
    return jnp.dot(a.astype(MXU_DTYPE), b.astype(MXU_DTYPE), preferred_element_type=f32)


def _mm_nt(a, b):
    return lax.dot_general(a.astype(MXU_DTYPE), b.astype(MXU_DTYPE), NT_DIMS, preferred_element_type=f32)


def _mm_tn(a, b):
    return lax.dot_general(a.astype(MXU_DTYPE), b.astype(MXU_DTYPE), TN_DIMS, preferred_element_type=f32)


def _rms(x, g):
    return x * lax.rsqrt(jnp.mean(x * x, axis=-1, keepdims=True) + EPS) * g


def _silu(x):
    return x * jax.nn.sigmoid(x)


def _softplus(x):
    return jnp.maximum(x, 0.0) + jnp.log1p(jnp.exp(-jnp.abs(x)))


def _neg_expm1(y):
    acc = jnp.full_like(y, 1.0 / 479001600.0)
    for k in (39916800.0, 3628800.0, 362880.0, 40320.0, 5040.0, 720.0, 120.0, 24.0, 6.0, 2.0, 1.0):
        acc = acc * y + 1.0 / k
    return jnp.where(y > -0.25, -(acc * y), 1.0 - jnp.exp(y))


def _sort_key(x):
    b = lax.bitcast_convert_type(x, jnp.int32)
    return b ^ ((b >> 31) & jnp.int32(0x7FFFFFFF))


def _softmax_update(s, v, m_ref, l_ref, acc_ref):
    m_old = m_ref[...]
    m_new = jnp.maximum(m_old, jnp.max(s, axis=1, keepdims=True))
    m_safe = jnp.where(m_new == -jnp.inf, 0.0, m_new)
    alpha = jnp.exp(m_old - m_safe)
    p = jnp.exp(s - m_safe)
    l_ref[...] = alpha * l_ref[...] + jnp.sum(p, axis=1, keepdims=True)
    acc_ref[...] = alpha * acc_ref[...] + _mm(p, v)
    m_ref[...] = m_new


def _softmax_init(m_ref, l_ref, acc_ref):
    m_ref[...] = jnp.full(m_ref.shape, -jnp.inf, f32)
    l_ref[...] = jnp.zeros(l_ref.shape, f32)
    acc_ref[...] = jnp.zeros(acc_ref.shape, f32)


def _stack_heads(x, n, w):
    return jnp.concatenate([x[:, h * w:(h + 1) * w] for h in range(n)], axis=0)


def _inproj_kernel(x_ref, g_ref, w_ref, o_ref, h_ref):
    @pl.when(pl.program_id(1) == 0)
    def _():
        h_ref[...] = _rms(x_ref[...], g_ref[...]).astype(h_ref.dtype)

    o_ref[...] = jnp.dot(h_ref[...], w_ref[...], preferred_element_type=f32)


def _inproj(x, gain, w):
    n, d = x.shape
    tm = min(n, 1024)
    tn = 512
    return pl.pallas_call(
        _inproj_kernel,
        grid=(n // tm, W_PAD // tn),
        in_specs=[pl.BlockSpec((tm, d), lambda i, j: (i, 0)),
                  pl.BlockSpec((1, d), lambda i, j: (0, 0)),
                  pl.BlockSpec((d, tn), lambda i, j: (0, j))],
        out_specs=pl.BlockSpec((tm, tn), lambda i, j: (i, j)),
        out_shape=jax.ShapeDtypeStruct((n, W_PAD), f32),
        scratch_shapes=[pltpu.VMEM((tm, d), MXU_DTYPE)],
        compiler_params=_cparams("parallel", "arbitrary"),
        name="inproj",
    )(x, gain, w)


def _prep_kernel(cq_ref, ckv_ref, kr_ref, dq_ref, dqi_ref, dkv_ref, dki_ref, cos_ref, sin_ref,
                 qn_ref, wuq_ref, kvn_ref, wuk_ref,
                 rows_ref, mk_ref, mq_ref, kvf_ref, kvb_ref, kif_ref, kib_ref, dqb_ref, dqib_ref):
    tm = cq_ref.shape[0]
    cos = cos_ref[...]
    sin = sin_ref[...]
    lane = lax.broadcasted_iota(jnp.int32, (tm, LANE), 1)
    first_half = (lane % MLA_ROPE) < (MLA_ROPE // 2)

    def rope(x):
        swapped = jnp.where(first_half, pltpu.roll(x, LANE - MLA_ROPE // 2, 1), pltpu.roll(x, MLA_ROPE // 2, 1))
        return x * cos + swapped * sin

    c = _rms(ckv_ref[...], kvn_ref[...])
    kr = rope(kr_ref[...])
    rows_ref[:, 0:MLA_KV_LORA] = c
    rows_ref[:, MLA_KV_LORA:MLA_ROW] = kr[:, 0:MLA_ROPE]
    mk_ref[:, 0:MLA_KV_LORA] = c.astype(mk_ref.dtype)
    mk_ref[:, MLA_KV_LORA:MLA_KPAD] = kr.astype(mk_ref.dtype)

    q = _mm(_rms(cq_ref[...], qn_ref[...]), wuq_ref[...])
    q_nope = q[:, 0:MLA_HEADS * MLA_NOPE]
    q_rope = rope(q[:, MLA_HEADS * MLA_NOPE:])
    lane_q = lax.broadcasted_iota(jnp.int32, q_nope.shape, 1)
    wuk = wuk_ref[...]
    for h in range(MLA_HEADS):
        q_h = jnp.where(lane_q // MLA_NOPE == h, q_nope, 0.0)
        q_lat = _mm(q_h, wuk)
        base = h * MLA_KPAD
        mq_ref[:, base:base + MLA_KV_LORA] = (q_lat * MLA_SCALE).astype(mq_ref.dtype)
        qr = q_rope if h == 0 else pltpu.roll(q_rope, LANE - h * MLA_ROPE, 1)
        qr = jnp.where(lane < MLA_ROPE, qr, 0.0)
        mq_ref[:, base + MLA_KV_LORA:base + MLA_KPAD] = (qr * MLA_SCALE).astype(mq_ref.dtype)

    dkv = dkv_ref[...]
    kvf_ref[...] = dkv
    kvb_ref[...] = dkv.astype(kvb_ref.dtype)
    seg = dki_ref[...]
    kif_ref[...] = seg[:, 0:DSA_IDX_DIM]
    kib_ref[...] = jnp.where(lane < DSA_IDX_DIM, seg, 0.0).astype(kib_ref.dtype)
    dqb_ref[...] = (dq_ref[...] * (DSA_HD ** -0.5)).astype(dqb_ref.dtype)
    dqib_ref[...] = (dqi_ref[...] * (DSA_IDX_DIM ** -0.5)).astype(dqib_ref.dtype)


def _prep(p, cos_t, sin_t, q_norm, w_uq, kv_norm, w_uk):
    n = p.shape[0]
    tm = min(n, 256)
    nt = cos_t.shape[0] // tm

    def seg(off, w):
        return pl.BlockSpec((tm, w), lambda i: (i, off // w))

    def const(a):
        return pl.BlockSpec(a.shape, lambda i: (0,) * a.ndim)

    def out(w):
        return pl.BlockSpec((tm, w), lambda i: (i, 0))

    outs = [(MLA_ROW, f32), (MLA_KPAD, MXU_DTYPE), (MLA_HEADS * MLA_KPAD, MXU_DTYPE),
            (2 * DSA_HD, f32), (2 * DSA_HD, MXU_DTYPE), (DSA_IDX_DIM, f32), (LANE, MXU_DTYPE),
            (DSA_HEADS * DSA_HD, MXU_DTYPE), (DSA_IDX_HEADS * LANE, MXU_DTYPE)]
    return pl.pallas_call(
        _prep_kernel,
        grid=(n // tm,),
        in_specs=[seg(OFF_CQ, 256), seg(OFF_CKV, 256), seg(OFF_KR, LANE), seg(OFF_DQ, 512),
                  seg(OFF_DQI, 512), seg(OFF_DKV, 256), seg(OFF_DKI, LANE),
                  pl.BlockSpec((tm, LANE), lambda i: (i % nt, 0)),
                  pl.BlockSpec((tm, LANE), lambda i: (i % nt, 0)),
                  const(q_norm), const(w_uq), const(kv_norm), const(w_uk)],
        out_specs=[out(w) for w, _ in outs],
        out_shape=[jax.ShapeDtypeStruct((n, w), dt) for w, dt in outs],
        compiler_params=_cparams("parallel"),
        name="prep",
    )(p, p, p, p, p, p, p, cos_t, sin_t, q_norm, w_uq, kv_norm, w_uk)


def _gla_gate(glr, wg2, bg):
    z = _mm(glr, wg2) + bg
    return (jnp.minimum(z, 0.0) - jnp.log1p(jnp.exp(-jnp.abs(z)))) * (1.0 / GLA_TAU)


def _gla_out(o, gate, gn):
    return _rms(o, gn) * _silu(gate)


def _gla_prompt_kernel(q_ref, k_ref, v_ref, gate_ref, glr_ref, wg2_ref, bg_ref, gn_ref,
                       o_ref, s_ref, st_ref, *, n_chunks):
    t = pl.program_id(1)

    @pl.when(t == 0)
    def _():
        st_ref[...] = jnp.zeros(st_ref.shape, f32)

    C = GLA_CHUNK
    HK = GLA_HEADS * GLA_DK
    r_i = lax.broadcasted_iota(jnp.int32, (C, C), 0)
    c_i = lax.broadcasted_iota(jnp.int32, (C, C), 1)
    causal = r_i >= c_i
    tril = causal.astype(f32)
    lane_head = lax.broadcasted_iota(jnp.int32, (C, HK), 1) // GLA_DK
    lane_head_st = lax.broadcasted_iota(jnp.int32, (GLA_DV, HK), 1) // GLA_DK
    causal_h = jnp.concatenate([causal] * GLA_HEADS, axis=0)
    gn = gn_ref[...]
    wg2 = wg2_ref[...]
    bg = bg_ref[...]
    for c in range(n_chunks):
        rows = slice(c * C, (c + 1) * C)
        g = _gla_gate(glr_ref[rows, :], wg2, bg)
        b = jnp.dot(tril, g, precision=lax.Precision.HIGHEST, preferred_element_type=f32)
        b_last = b[C - 1:C, :]
        q = q_ref[rows, :] * (GLA_DK ** -0.5)
        k = k_ref[rows, :]
        v = v_ref[rows, :]
        q_dec = q * jnp.exp(b)
        k_dec = k * jnp.exp(-b)
        k_tail = k * jnp.exp(b_last - b)
        decay = jnp.exp(b_last)
        q_heads = jnp.concatenate(
            [jnp.where(lane_head == h, q_dec, 0.0) for h in range(GLA_HEADS)], axis=0)
        att = jnp.where(causal_h, _mm_nt(q_heads, k_dec), 0.0)
        st = st_ref[...]
        o_inter = _mm_nt(q_heads, st)
        gate = gate_ref[rows, :]
        for h in range(GLA_HEADS):
            hv = slice(h * GLA_DV, (h + 1) * GLA_DV)
            o_h = _mm(att[h * C:(h + 1) * C, :], v[:, hv]) + o_inter[h * C:(h + 1) * C, :]
            o_ref[rows, hv] = _gla_out(o_h, gate[:, hv], gn)
        kv = _mm_tn(v, k_tail)
        upd = jnp.zeros((GLA_DV, HK), f32)
        for h in range(GLA_HEADS):
            upd = upd + jnp.where(lane_head_st == h, kv[h * GLA_DV:(h + 1) * GLA_DV, :], 0.0)
        st_ref[...] = decay * st + upd

    @pl.when(t == pl.num_programs(1) - 1)
    def _():
        s_ref[...] = st_ref[...].T.reshape(GLA_HEADS, GLA_DK, GLA_DV)


def _gla_prompt(p, B, L, wg2, bg, gn):
    tl = min(L, 256)
    nt = L // tl

    def seg(off, w):
        return pl.BlockSpec((tl, w), lambda b, t: (b * nt + t, off // w))

    def const(a):
        return pl.BlockSpec(a.shape, lambda b, t: (0,) * a.ndim)

    return pl.pallas_call(
        functools.partial(_gla_prompt_kernel, n_chunks=tl // GLA_CHUNK),
        grid=(B, nt),
        in_specs=[seg(OFF_GQ, 256), seg(OFF_GK, 256), seg(OFF_GV, 512), seg(OFF_GGATE, 512),
                  seg(OFF_GLR, LANE), const(wg2), const(bg), const(gn)],
        out_specs=[pl.BlockSpec((tl, BRANCH_W), lambda b, t: (b * nt + t, 0)),
                   pl.BlockSpec((None, GLA_HEADS, GLA_DK, GLA_DV), lambda b, t: (b, 0, 0, 0))],
        out_shape=[jax.ShapeDtypeStruct((B * L, BRANCH_W), f32),
                   jax.ShapeDtypeStruct((B, GLA_HEADS, GLA_DK, GLA_DV), f32)],
        scratch_shapes=[pltpu.VMEM((GLA_DV, GLA_HEADS * GLA_DK), f32)],
        compiler_params=_cparams("parallel", "arbitrary"),
        name="gla_prompt",
    )(p, p, p, p, p, wg2, bg, gn)


def _gla_sample_kernel(*refs, n_tok, bg_rows):
    q_refs = refs[0:n_tok]
    k_refs = refs[n_tok:2 * n_tok]
    v_refs = refs[2 * n_tok:3 * n_tok]
    gate_refs = refs[3 * n_tok:4 * n_tok]
    glr_refs = refs[4 * n_tok:5 * n_tok]
    s0_ref, wg2_ref, bg_ref, gn_ref, o_ref, s_ref, tr_ref, oraw_ref = refs[5 * n_tok:]
    HK = GLA_HEADS * GLA_DK
    zpad = jnp.zeros((LANE - bg_rows, HK), f32)

    def col_major(x):
        return jnp.concatenate([x, zpad], axis=0).T

    for t in range(n_tok):
        g = _gla_gate(glr_refs[t][...], wg2_ref[...], bg_ref[...])
        tr_ref[3 * t + 0] = col_major(q_refs[t][...] * (GLA_DK ** -0.5))
        tr_ref[3 * t + 1] = col_major(k_refs[t][...])
        tr_ref[3 * t + 2] = col_major(jnp.exp(g))
    for i in range(bg_rows):
        s = s0_ref[i].reshape(HK, GLA_DV)
        for t in range(n_tok):
            qc = tr_ref[3 * t + 0, :, i:i + 1]
            kc = tr_ref[3 * t + 1, :, i:i + 1]
            ec = tr_ref[3 * t + 2, :, i:i + 1]
            vrow = v_refs[t][i:i + 1, :]
            vb = jnp.concatenate(
                [jnp.broadcast_to(vrow[:, h * GLA_DV:(h + 1) * GLA_DV], (GLA_DK, GLA_DV))
                 for h in range(GLA_HEADS)], axis=0)
            s = ec * s + kc * vb
            prod = qc * s
            oraw_ref[t, i:i + 1, :] = jnp.concatenate(
                [jnp.sum(prod[h * GLA_DK:(h + 1) * GLA_DK, :], axis=0, keepdims=True)
                 for h in range(GLA_HEADS)], axis=1)
        s_ref[i] = s.reshape(GLA_HEADS, GLA_DK, GLA_DV)
    gn = gn_ref[...]
    for t in range(n_tok):
        o = oraw_ref[t]
        gate = gate_refs[t][...]
        for h in range(GLA_HEADS):
            hv = slice(h * GLA_DV, (h + 1) * GLA_DV)
            o_ref[:, t * BRANCH_W + h * GLA_DV:t * BRANCH_W + (h + 1) * GLA_DV] = _gla_out(o[:, hv], gate[:, hv], gn)
    o_ref[:, n_tok * BRANCH_W:] = jnp.zeros((bg_rows, (SUB - n_tok) * BRANCH_W), f32)


def _gla_sample(p2, s0, n_tok, wg2, bg, gn):
    db = p2.shape[0]
    bgr = min(db, 16)

    def seg(t, off, w):
        return pl.BlockSpec((bgr, w), lambda i: (i, (t * W_PAD + off) // w))

    def const(a):
        return pl.BlockSpec(a.shape, lambda i: (0,) * a.ndim)

    in_specs = ([seg(t, OFF_GQ, 256) for t in range(n_tok)] + [seg(t, OFF_GK, 256) for t in range(n_tok)]
                + [seg(t, OFF_GV, 512) for t in range(n_tok)] + [seg(t, OFF_GGATE, 512) for t in range(n_tok)]
                + [seg(t, OFF_GLR, LANE) for t in range(n_tok)]
                + [pl.BlockSpec((bgr, GLA_HEADS, GLA_DK, GLA_DV), lambda i: (i, 0, 0, 0)),
                   const(wg2), const(bg), const(gn)])
    return pl.pallas_call(
        functools.partial(_gla_sample_kernel, n_tok=n_tok, bg_rows=bgr),
        grid=(db // bgr,),
        in_specs=in_specs,
        out_specs=[pl.BlockSpec((bgr, SUB * BRANCH_W), lambda i: (i, 0)),
                   pl.BlockSpec((bgr, GLA_HEADS, GLA_DK, GLA_DV), lambda i: (i, 0, 0, 0))],
        out_shape=[jax.ShapeDtypeStruct((db, SUB * BRANCH_W), f32),
                   jax.ShapeDtypeStruct(s0.shape, f32)],
        scratch_shapes=[pltpu.VMEM((3 * n_tok, GLA_HEADS * GLA_DK, LANE), f32),
                        pltpu.VMEM((n_tok, bgr, BRANCH_W), f32)],
        compiler_params=_cparams("parallel"),
        name="gla_sample",
    )(*([p2] * (5 * n_tok)), s0, wg2, bg, gn)


def _lru_gates(xc, wr, br, wi, bi, sp):
    r = jax.nn.sigmoid(_mm(xc, wr) + br)
    ig = jax.nn.sigmoid(_mm(xc, wi) + bi)
    log_a = -LRU_C * r * sp
    return jnp.exp(log_a), jnp.sqrt(_neg_expm1(2.0 * log_a)) * (ig * xc)


def _lru_prompt_kernel(x_ref, gate_ref, cw_ref, cb_ref, wr_ref, br_ref, wi_ref, bi_ref, lam_ref,
                       o_ref, hfin_ref, conv_ref, tail_ref, h_ref, a_ref, u_ref, hs_ref):
    t = pl.program_id(1)
    tl = x_ref.shape[0]

    @pl.when(t == 0)
    def _():
        tail_ref[...] = jnp.zeros(tail_ref.shape, f32)
        h_ref[...] = jnp.zeros(h_ref.shape, f32)

    x = x_ref[...]
    ext = jnp.concatenate([tail_ref[...], x], axis=0)
    cw = cw_ref[...]
    xc = jnp.zeros(x.shape, f32)
    for j in range(CONV_W):
        s = SUB - (CONV_W - 1) + j
        xc = xc + ext[s:s + tl, :] * cw[j:j + 1, :]
    xc = cb_ref[...] + xc
    a, u = _lru_gates(xc, wr_ref[...], br_ref[...], wi_ref[...], bi_ref[...], _softplus(-lam_ref[...]))
    a_ref[...] = a
    u_ref[...] = u

    def step(i, h):
        h = a_ref[pl.ds(i, 1), :] * h + u_ref[pl.ds(i, 1), :]
        hs_ref[pl.ds(i, 1), :] = h
        return h

    h = lax.fori_loop(0, tl, step, h_ref[...], unroll=8)
    h_ref[...] = h
    tail_ref[...] = x[tl - SUB:, :]
    o_ref[...] = hs_ref[...] * _silu(gate_ref[...])

    @pl.when(t == pl.num_programs(1) - 1)
    def _():
        hfin_ref[...] = h
        conv_ref[...] = x[tl - (CONV_W - 1):, :]


def _lru_prompt(p, B, L, cw, cb, wr, br, wi, bi, lam):
    tl = min(L, 256)
    nt = L // tl

    def seg(off, w):
        return pl.BlockSpec((tl, w), lambda b, t: (b * nt + t, off // w))

    def const(a):
        return pl.BlockSpec(a.shape, lambda b, t: (0,) * a.ndim)

    return pl.pallas_call(
        _lru_prompt_kernel,
        grid=(B, nt),
        in_specs=[seg(OFF_LX, 512), seg(OFF_LGATE, 512), const(cw), const(cb), const(wr), const(br),
                  const(wi), const(bi), const(lam)],
        out_specs=[pl.BlockSpec((tl, LRU_W), lambda b, t: (b * nt + t, 0)),
                   pl.BlockSpec((None, 1, LRU_W), lambda b, t: (b, 0, 0)),
                   pl.BlockSpec((None, CONV_W - 1, LRU_W), lambda b, t: (b, 0, 0))],
        out_shape=[jax.ShapeDtypeStruct((B * L, LRU_W), f32),
                   jax.ShapeDtypeStruct((B, 1, LRU_W), f32),
                   jax.ShapeDtypeStruct((B, CONV_W - 1, LRU_W), f32)],
        scratch_shapes=[pltpu.VMEM((SUB, LRU_W), f32), pltpu.VMEM((1, LRU_W), f32),
                        pltpu.VMEM((tl, LRU_W), f32), pltpu.VMEM((tl, LRU_W), f32),
                        pltpu.VMEM((tl, LRU_W), f32)],
        compiler_params=_cparams("parallel", "arbitrary"),
        name="lru_prompt",
    )(p, p, cw, cb, wr, br, wi, bi, lam)


def _lru_sample_kernel(*refs, n_tok):
    x_refs = refs[0:n_tok]
    gate_refs = refs[n_tok:2 * n_tok]
    (buf_ref, h0_ref, cw_ref, cb_ref, wr_ref, br_ref, wi_ref, bi_ref, lam_ref,
     o_ref, hfin_ref, conv_ref) = refs[2 * n_tok:]
    W = LRU_W
    xs = [buf_ref[:, j * W:(j + 1) * W] for j in range(CONV_W - 1)] + [r[...] for r in x_refs]
    cw = cw_ref[...]
    sp = _softplus(-lam_ref[...])
    h = h0_ref[...]
    for t in range(n_tok):
        xc = jnp.zeros(h.shape, f32)
        for j in range(CONV_W):
            xc = xc + xs[t + j] * cw[j:j + 1, :]
        xc = cb_ref[...] + xc
        a, u = _lru_gates(xc, wr_ref[...], br_ref[...], wi_ref[...], bi_ref[...], sp)
        h = a * h + u
        o_ref[:, t * W:(t + 1) * W] = h * _silu(gate_refs[t][...])
    o_ref[:, n_tok * W:] = jnp.zeros((h.shape[0], (SUB - n_tok) * W), f32)
    hfin_ref[...] = h
    for j in range(CONV_W - 1):
        conv_ref[:, j * W:(j + 1) * W] = xs[n_tok + j]


def _lru_sample(p2, buf, h0, n_tok, cw, cb, wr, br, wi, bi, lam):
    db = p2.shape[0]

    def seg(t, off, w):
        return pl.BlockSpec((db, w), lambda i: (0, (t * W_PAD + off) // w))

    def const(a):
        return pl.BlockSpec(a.shape, lambda i: (0,) * a.ndim)

    in_specs = ([seg(t, OFF_LX, 512) for t in range(n_tok)] + [seg(t, OFF_LGATE, 512) for t in range(n_tok)]
                + [const(a) for a in (buf, h0, cw, cb, wr, br, wi, bi, lam)])
    return pl.pallas_call(
        functools.partial(_lru_sample_kernel, n_tok=n_tok),
        grid=(1,),
        in_specs=in_specs,
        out_specs=[pl.BlockSpec((db, SUB * LRU_W), lambda i: (0, 0)),
                   pl.BlockSpec((db, LRU_W), lambda i: (0, 0)),
                   pl.BlockSpec((db, (CONV_W - 1) * LRU_W), lambda i: (0, 0))],
        out_shape=[jax.ShapeDtypeStruct((db, SUB * LRU_W), f32),
                   jax.ShapeDtypeStruct((db, LRU_W), f32),
                   jax.ShapeDtypeStruct((db, (CONV_W - 1) * LRU_W), f32)],
        compiler_params=_cparams("arbitrary"),
        name="lru_sample",
    )(*([p2] * (2 * n_tok)), buf, h0, cw, cb, wr, br, wi, bi, lam)


def _mla_finish(acc_ref, l_ref, wuv_ref, gate, o_ref, rows):
    o = acc_ref[...] / l_ref[...]
    for h in range(MLA_HEADS):
        oc = _mm(o[h * rows:(h + 1) * rows, :], wuv_ref[h])
        o_ref[:, h * MLA_V:(h + 1) * MLA_V] = oc * _silu(gate[:, h * MLA_V:(h + 1) * MLA_V])


def _mla_prompt_kernel(q_ref, k_ref, wuv_ref, gate_ref, o_ref, m_ref, l_ref, acc_ref, *, tq, tk):
    i = pl.program_id(1)
    qs = _stack_heads(q_ref[...], MLA_HEADS, MLA_KPAD)
    _softmax_init(m_ref, l_ref, acc_ref)
    n_full = (i * tq) // tk

    def body(j, carry):
        k = k_ref[pl.ds(pl.multiple_of(j * tk, tk), tk), :]
        _softmax_update(_mm_nt(qs, k), k[:, 0:MLA_KV_LORA], m_ref, l_ref, acc_ref)
        return carry

    lax.fori_loop(0, n_full, body, 0)
    k = k_ref[pl.ds(pl.multiple_of(n_full * tk, tk), tk), :]
    s = _mm_nt(qs, k)
    row = i * tq + lax.broadcasted_iota(jnp.int32, s.shape, 0) % tq
    col = n_full * tk + lax.broadcasted_iota(jnp.int32, s.shape, 1)
    _softmax_update(jnp.where(col <= row, s, -jnp.inf), k[:, 0:MLA_KV_LORA], m_ref, l_ref, acc_ref)
    _mla_finish(acc_ref, l_ref, wuv_ref, gate_ref[...], o_ref, tq)


def _mla_prompt(mq, mk, p, B, L, wuv):
    tq = 128
    tk = min(L, 512)
    nq = L // tq
    rows = MLA_HEADS * tq
    return pl.pallas_call(
        functools.partial(_mla_prompt_kernel, tq=tq, tk=tk),
        grid=(B, nq),
        in_specs=[pl.BlockSpec((tq, MLA_HEADS * MLA_KPAD), lambda b, i: (b * nq + i, 0)),
                  pl.BlockSpec((L, MLA_KPAD), lambda b, i: (b, 0)),
                  pl.BlockSpec(wuv.shape, lambda b, i: (0, 0, 0)),
                  pl.BlockSpec((tq, BRANCH_W), lambda b, i: (b * nq + i, OFF_CGATE // BRANCH_W))],
        out_specs=pl.BlockSpec((tq, BRANCH_W), lambda b, i: (b * nq + i, 0)),
        out_shape=jax.ShapeDtypeStruct((B * L, BRANCH_W), f32),
        scratch_shapes=[pltpu.VMEM((rows, 1), f32), pltpu.VMEM((rows, 1), f32),
                        pltpu.VMEM((rows, MLA_KV_LORA), f32)],
        compiler_params=_cparams("parallel", "arbitrary"),
        name="mla_prompt",
    )(mq, mk, wuv, p)


def _new_block(x):
    return jnp.concatenate([x.astype(f32), jnp.zeros((LANE - SUB, x.shape[1]), f32)], axis=0)


def _mla_sample_kernel(pt_ref, q_ref, knew_ref, wuv_ref, gate_ref, *rest, n_grp):
    page_refs = rest[0:n_grp]
    o_ref, kbuf_ref, m_ref, l_ref, acc_ref = rest[n_grp:]
    b = pl.program_id(0)
    j = pl.program_id(1)

    @pl.when((b == 0) & (j == 0))
    def _():
        kbuf_ref[...] = jnp.zeros(kbuf_ref.shape, kbuf_ref.dtype)

    @pl.when(j == 0)
    def _():
        _softmax_init(m_ref, l_ref, acc_ref)

    qs = _stack_heads(q_ref[...], MLA_HEADS, MLA_KPAD)
    for g in range(n_grp):
        kbuf_ref[g * PAGE_SIZE:(g + 1) * PAGE_SIZE, 0:MLA_ROW] = page_refs[g][...].astype(kbuf_ref.dtype)
    k = kbuf_ref[...]
    _softmax_update(_mm_nt(qs, k), k[:, 0:MLA_KV_LORA], m_ref, l_ref, acc_ref)

    @pl.when(j == pl.num_programs(1) - 1)
    def _():
        kn = _new_block(knew_ref[...]).astype(MXU_DTYPE)
        s = _mm_nt(qs, kn)
        tok = lax.broadcasted_iota(jnp.int32, s.shape, 0) % SUB
        col = lax.broadcasted_iota(jnp.int32, s.shape, 1)
        _softmax_update(jnp.where(col <= tok, s, -jnp.inf), kn[:, 0:MLA_KV_LORA], m_ref, l_ref, acc_ref)
        _mla_finish(acc_ref, l_ref, wuv_ref, gate_ref[...], o_ref, SUB)


def _mla_sample(page_table, cache, layer, mq3, mk3, p3, wuv):
    db, n_pages = page_table.shape
    n_grp = math.gcd(n_pages, 8)
    n_steps = n_pages // n_grp
    rows = MLA_HEADS * SUB

    def page_spec(g):
        return pl.BlockSpec((None, None, PAGE_SIZE, MLA_ROW),
                            lambda b, j, pt: (layer, pt[b * n_pages + j * n_grp + g], 0, 0))

    grid_spec = pltpu.PrefetchScalarGridSpec(
        num_scalar_prefetch=1,
        grid=(db, n_steps),
        in_specs=[pl.BlockSpec((None, SUB, MLA_HEADS * MLA_KPAD), lambda b, j, pt: (b, 0, 0)),
                  pl.BlockSpec((None, SUB, MLA_KPAD), lambda b, j, pt: (b, 0, 0)),
                  pl.BlockSpec(wuv.shape, lambda b, j, pt: (0, 0, 0)),
                  pl.BlockSpec((None, SUB, BRANCH_W), lambda b, j, pt: (b, 0, OFF_CGATE // BRANCH_W))]
                 + [page_spec(g) for g in range(n_grp)],
        out_specs=pl.BlockSpec((None, SUB, BRANCH_W), lambda b, j, pt: (b, 0, 0)),
        scratch_shapes=[pltpu.VMEM((n_grp * PAGE_SIZE, MLA_KPAD), MXU_DTYPE),
                        pltpu.VMEM((rows, 1), f32), pltpu.VMEM((rows, 1), f32),
                        pltpu.VMEM((rows, MLA_KV_LORA), f32)])
    return pl.pallas_call(
        functools.partial(_mla_sample_kernel, n_grp=n_grp),
        grid_spec=grid_spec,
        out_shape=jax.ShapeDtypeStruct((db, SUB, BRANCH_W), f32),
        compiler_params=_cparams("arbitrary", "arbitrary"),
        name="mla_sample",
    )(page_table.reshape(-1), mq3, mk3, wuv, p3, *([cache] * n_grp))


def _index_scores(qi_stack, wcols, keys_b, rows):
    sc = jnp.maximum(_mm_nt(qi_stack, keys_b), 0.0)
    tot = sc[0:rows, :] * wcols[0]
    for h in range(1, DSA_IDX_HEADS):
        tot = tot + sc[h * rows:(h + 1) * rows, :] * wcols[h]
    return jnp.where(tot == 0.0, 0.0, tot)


def _wi_cols(seg):
    scale = DSA_IDX_HEADS ** -0.5
    return [seg[:, DSA_IDX_DIM + h:DSA_IDX_DIM + h + 1] * scale for h in range(DSA_IDX_HEADS)]


def _kth_largest(count_ge, rows, topk):
    t0 = jnp.where(count_ge(jnp.zeros((rows, 1), jnp.int32)) >= topk,
                   jnp.int32(0), jnp.int32(INT_MIN))

    def bit_body(it, t):
        cand = t + lax.shift_left(jnp.int32(1), jnp.int32(30) - it)
        return jnp.where(count_ge(cand) >= topk, cand, t)

    return lax.fori_loop(0, 31, bit_body, t0)


def _dsa_finish(acc_ref, l_ref, gate, o_ref, rows):
    o = acc_ref[...] / l_ref[...]
    for h in range(DSA_HEADS):
        hv = slice(h * DSA_HD, (h + 1) * DSA_HD)
        o_ref[:, hv] = o[h * rows:(h + 1) * rows, :] * _silu(gate[:, hv])


def _dsa_prompt_kernel(q_ref, qi_ref, wi_ref, gate_ref, kib_ref, kvb_ref, u_ref, o_ref,
                       keys_ref, m_ref, l_ref, acc_ref, *, tq, tk, topk):
    i = pl.program_id(1)
    n_ch = (i * tq) // tk + 1
    qi = _stack_heads(qi_ref[...], DSA_IDX_HEADS, LANE)
    wcols = _wi_cols(wi_ref[...])
    row_pos = i * tq + lax.broadcasted_iota(jnp.int32, (tq, tk), 0)
    col_in = lax.broadcasted_iota(jnp.int32, (tq, tk), 1)

    def score_body(j, carry):
        kb = kib_ref[pl.ds(pl.multiple_of(j * tk, tk), tk), :]
        tot = _index_scores(qi, wcols, kb, tq)
        keys_ref[j] = _sort_key(jnp.where(j * tk + col_in <= row_pos, tot, -jnp.inf))
        return carry

    lax.fori_loop(0, n_ch, score_body, 0)

    def count(pred):
        def body(j, c):
            return c + jnp.sum(pred(keys_ref[j]).astype(f32), axis=1, keepdims=True)
        return lax.fori_loop(0, n_ch, body, jnp.zeros((tq, 1), f32))

    thr = _kth_largest(lambda cand: count(lambda kk: kk >= cand), tq, topk)
    need = topk - count(lambda kk: kk > thr)
    tie_ok = thr > KEY_NEG_INF
    qs = _stack_heads(q_ref[...], DSA_HEADS, DSA_HD)
    _softmax_init(m_ref, l_ref, acc_ref)

    def att_body(j, run):
        kk = keys_ref[j]
        eq = (kk == thr) & tie_ok
        eqf = eq.astype(f32)
        before = _mm(eqf, u_ref[...]) + run
        sel = (kk > thr) | (eq & (before < need))
        kv = kvb_ref[pl.ds(pl.multiple_of(j * tk, tk), tk), :]
        s = _mm_nt(qs, kv[:, 0:DSA_HD])
        s = jnp.where(jnp.concatenate([sel] * DSA_HEADS, axis=0), s, -jnp.inf)
        _softmax_update(s, kv[:, DSA_HD:], m_ref, l_ref, acc_ref)
        return run + jnp.sum(eqf, axis=1, keepdims=True)

    lax.fori_loop(0, n_ch, att_body, jnp.zeros((tq, 1), f32))
    _dsa_finish(acc_ref, l_ref, gate_ref[...], o_ref, tq)


def _dsa_prompt(dqb, dqib, kib, kvb, p, B, L, topk):
    tq = 128
    tk = min(L, 512)
    nq = L // tq
    rows = DSA_HEADS * tq
    upper = (np.arange(tk)[:, None] < np.arange(tk)[None, :]).astype(np.float32)
    u = jnp.asarray(upper, MXU_DTYPE)
    return pl.pallas_call(
        functools.partial(_dsa_prompt_kernel, tq=tq, tk=tk, topk=topk),
        grid=(B, nq),
        in_specs=[pl.BlockSpec((tq, DSA_HEADS * DSA_HD), lambda b, i: (b * nq + i, 0)),
                  pl.BlockSpec((tq, DSA_IDX_HEADS * LANE), lambda b, i: (b * nq + i, 0)),
                  pl.BlockSpec((tq, LANE), lambda b, i: (b * nq + i, OFF_DKI // LANE)),
                  pl.BlockSpec((tq, BRANCH_W), lambda b, i: (b * nq + i, OFF_DGATE // BRANCH_W)),
                  pl.BlockSpec((L, LANE), lambda b, i: (b, 0)),
                  pl.BlockSpec((L, 2 * DSA_HD), lambda b, i: (b, 0)),
                  pl.BlockSpec((tk, tk), lambda b, i: (0, 0))],
        out_specs=pl.BlockSpec((tq, BRANCH_W), lambda b, i: (b * nq + i, 0)),
        out_shape=jax.ShapeDtypeStruct((B * L, BRANCH_W), f32),
        scratch_shapes=[pltpu.VMEM((L // tk, tq, tk), jnp.int32),
                        pltpu.VMEM((rows, 1), f32), pltpu.VMEM((rows, 1), f32),
                        pltpu.VMEM((rows, DSA_HD), f32)],
        compiler_params=_cparams("parallel", "arbitrary"),
        name="dsa_prompt",
    )(dqb, dqib, p, p, kib, kvb, u)


def _lane_prefix(x):
    n = x.shape[1]
    lane = lax.broadcasted_iota(jnp.int32, x.shape, 1)
    s = 1
    while s < n:
        x = x + jnp.where(lane >= s, pltpu.roll(x, s, 1), 0.0)
        s *= 2
    return x


def _dsa_sample_kernel(pt_ref, q_ref, qi_ref, seg_ref, kvnew_ref, gate_ref, *rest, n_grp, n_steps, topk):
    ki_refs = rest[0:n_grp]
    kv_refs = rest[n_grp:2 * n_grp]
    (o_ref, kibuf_ref, keys_ref, knew_ref, thr_ref, need_ref, run_ref, m_ref, l_ref, acc_ref) = rest[2 * n_grp:]
    b = pl.program_id(0)
    j = pl.program_id(1)
    nk = n_grp * PAGE_SIZE

    @pl.when((b == 0) & (j == 0))
    def _():
        kibuf_ref[...] = jnp.zeros(kibuf_ref.shape, kibuf_ref.dtype)

    seg = seg_ref[...]
    wcols = _wi_cols(seg)

    @pl.when(j < n_steps)
    def _():
        qi = _stack_heads(qi_ref[...], DSA_IDX_HEADS, LANE)
        for g in range(n_grp):
            kibuf_ref[g * PAGE_SIZE:(g + 1) * PAGE_SIZE, 0:DSA_IDX_DIM] = ki_refs[g][...].astype(kibuf_ref.dtype)
        keys_ref[j] = _sort_key(_index_scores(qi, wcols, kibuf_ref[...], SUB))

        @pl.when(j == n_steps - 1)
        def _():
            lane = lax.broadcasted_iota(jnp.int32, (LANE, LANE), 1)
            kn = jnp.where(lane < DSA_IDX_DIM, _new_block(seg), 0.0)
            tot = _index_scores(qi, wcols, kn, SUB)
            tok = lax.broadcasted_iota(jnp.int32, tot.shape, 0)
            col = lax.broadcasted_iota(jnp.int32, tot.shape, 1)
            knew_ref[...] = _sort_key(jnp.where(col <= tok, tot, -jnp.inf))

    @pl.when(j == n_steps)
    def _():
        def count(pred):
            def body(c_i, c):
                return c + jnp.sum(pred(keys_ref[c_i]).astype(f32), axis=1, keepdims=True)
            c0 = jnp.sum(pred(knew_ref[...]).astype(f32), axis=1, keepdims=True)
            return lax.fori_loop(0, n_steps, body, c0)

        thr = _kth_largest(lambda cand: count(lambda kk: kk >= cand), SUB, topk)
        thr_ref[...] = thr
        need_ref[...] = topk - count(lambda kk: kk > thr)
        run_ref[...] = jnp.zeros(run_ref.shape, f32)
        _softmax_init(m_ref, l_ref, acc_ref)

    def select(kk):
        thr = thr_ref[...]
        eq = (kk == thr) & (thr > KEY_NEG_INF)
        eqf = eq.astype(f32)
        incl = _lane_prefix(eqf)
        sel = (kk > thr) | (eq & (run_ref[...] + incl - eqf < need_ref[...]))
        run_ref[...] = run_ref[...] + incl[:, -1:]
        return jnp.concatenate([sel] * DSA_HEADS, axis=0)

    @pl.when(j >= n_steps)
    def _():
        qs = _stack_heads(q_ref[...], DSA_HEADS, DSA_HD)
        sel = select(keys_ref[j - n_steps])
        kv = jnp.concatenate([r[...].astype(MXU_DTYPE) for r in kv_refs], axis=0)
        s = jnp.where(sel, _mm_nt(qs, kv[:, 0:DSA_HD]), -jnp.inf)
        _softmax_update(s, kv[:, DSA_HD:], m_ref, l_ref, acc_ref)

        @pl.when(j == 2 * n_steps - 1)
        def _():
            sel_n = select(knew_ref[...])
            kvn = _new_block(kvnew_ref[...]).astype(MXU_DTYPE)
            s_n = jnp.where(sel_n, _mm_nt(qs, kvn[:, 0:DSA_HD]), -jnp.inf)
            _softmax_update(s_n, kvn[:, DSA_HD:], m_ref, l_ref, acc_ref)
            _dsa_finish(acc_ref, l_ref, gate_ref[...], o_ref, SUB)


def _dsa_sample(page_table, cache_ki, cache_kv, layer, dqb3, dqib3, p3, topk):
    db, n_pages = page_table.shape
    n_grp = math.gcd(n_pages, 8)
    n_steps = n_pages // n_grp
    rows = DSA_HEADS * SUB
    nk = n_grp * PAGE_SIZE

    def ki_spec(g):
        return pl.BlockSpec(
            (None, None, PAGE_SIZE, DSA_IDX_DIM),
            lambda b, j, pt: (layer, pt[b * n_pages + jnp.minimum(j, n_steps - 1) * n_grp + g], 0, 0))

    def kv_spec(g):
        return pl.BlockSpec(
            (None, None, PAGE_SIZE, 2 * DSA_HD),
            lambda b, j, pt: (layer, pt[b * n_pages + jnp.maximum(j - n_steps, 0) * n_grp + g], 0, 0))

    def row3(w, col):
        return pl.BlockSpec((None, SUB, w), lambda b, j, pt: (b, 0, col))

    grid_spec = pltpu.PrefetchScalarGridSpec(
        num_scalar_prefetch=1,
        grid=(db, 2 * n_steps),
        in_specs=[row3(DSA_HEADS * DSA_HD, 0), row3(DSA_IDX_HEADS * LANE, 0),
                  row3(LANE, OFF_DKI // LANE), row3(2 * DSA_HD, OFF_DKV // (2 * DSA_HD)),
                  row3(BRANCH_W, OFF_DGATE // BRANCH_W)]
                 + [ki_spec(g) for g in range(n_grp)] + [kv_spec(g) for g in range(n_grp)],
        out_specs=pl.BlockSpec((None, SUB, BRANCH_W), lambda b, j, pt: (b, 0, 0)),
        scratch_shapes=[pltpu.VMEM((nk, LANE), MXU_DTYPE),
                        pltpu.VMEM((n_steps, SUB, nk), jnp.int32),
                        pltpu.VMEM((SUB, LANE), jnp.int32),
                        pltpu.VMEM((SUB, 1), jnp.int32), pltpu.VMEM((SUB, 1), f32), pltpu.VMEM((SUB, 1), f32),
                        pltpu.VMEM((rows, 1), f32), pltpu.VMEM((rows, 1), f32),
                        pltpu.VMEM((rows, DSA_HD), f32)])
    return pl.pallas_call(
        functools.partial(_dsa_sample_kernel, n_grp=n_grp, n_steps=n_steps, topk=topk),
        grid_spec=grid_spec,
        out_shape=jax.ShapeDtypeStruct((db, SUB, BRANCH_W), f32),
        compiler_params=_cparams("arbitrary", "arbitrary"),
        name="dsa_sample",
    )(page_table.reshape(-1), dqb3, dqib3, p3, p3, p3, *([cache_ki] * n_grp), *([cache_kv] * n_grp))


def _merge_kernel(x_ref, oa_ref, ob_ref, oc_ref, od_ref, mg_ref, wb_ref, wo_ref, fg_ref, y_ref, *, final):
    mixed = jnp.zeros(x_ref.shape, f32)
    for n, o_ref in enumerate((oa_ref, ob_ref, oc_ref, od_ref)):
        gate = jax.nn.sigmoid(mg_ref[:, n * D_MODEL:(n + 1) * D_MODEL])
        mixed = mixed + gate * _mm(o_ref[...], wb_ref[n])
    y = x_ref[...] + _mm(mixed, wo_ref[...])
    y_ref[...] = _rms(y, fg_ref[...]) if final else y


def _merge(x, oa, ob, oc, od, p, wb, wo, fg, final):
    n = x.shape[0]
    tm = min(n, 256)

    def row(w):
        return pl.BlockSpec((tm, w), lambda i: (i, 0))

    def const(a):
        return pl.BlockSpec(a.shape, lambda i: (0,) * a.ndim)

    return pl.pallas_call(
        functools.partial(_merge_kernel, final=final),
        grid=(n // tm,),
        in_specs=[row(D_MODEL), row(BRANCH_W), row(BRANCH_W), row(BRANCH_W), row(BRANCH_W),
                  pl.BlockSpec((tm, N_BRANCH * D_MODEL), lambda i: (i, OFF_MERGE)),
                  const(wb), const(wo), const(fg)],
        out_specs=row(D_MODEL),
        out_shape=jax.ShapeDtypeStruct((n, D_MODEL), f32),
        compiler_params=_cparams("parallel"),
        name="merge",
    )(x, oa, ob, oc, od, p, wb, wo, fg)


def _pad_cols(w, width):
    return jnp.pad(w, ((0, 0), (0, width - w.shape[1])))


def _layout_w_in(w):
    def col(i):
        return w[:, _IN_OFFS[i]:_IN_OFFS[i + 1]]

    dqi = col(_DQI)
    dqi_pad = jnp.concatenate(
        [_pad_cols(dqi[:, h * DSA_IDX_DIM:(h + 1) * DSA_IDX_DIM], LANE) for h in range(DSA_IDX_HEADS)], axis=1)
    parts = [col(_MERGE), col(_GV), col(_GGATE), col(_LX), col(_LGATE), col(_CGATE), col(_DQ), col(_DGATE),
             dqi_pad, col(_GQ), col(_GK), col(_CQ), col(_CKV), col(_DK), col(_DV),
             _pad_cols(col(_GLR), LANE), _pad_cols(col(_KR), LANE),
             _pad_cols(jnp.concatenate([col(_DKI), col(_DWI)], axis=1), LANE)]
    return _pad_cols(jnp.concatenate(parts, axis=1), W_PAD).astype(MXU_DTYPE)


def _rope_tables(pos):
    half = MLA_ROPE // 2
    inv = ROPE_THETA ** (-jnp.arange(half, dtype=f32) / half)
    ang = pos.astype(f32)[:, None] * inv[None, :]
    lane = np.arange(LANE)
    sign = np.where((lane % MLA_ROPE) < half, -1.0, 1.0).astype(np.float32)
    return jnp.cos(ang)[:, lane % half], jnp.sin(ang)[:, lane % half] * sign


def _block_diag(w):
    eye = jnp.eye(LRU_BLOCKS, dtype=w.dtype)
    return jnp.einsum('nde,nm->ndme', w, eye).reshape(LRU_W, LRU_W).astype(MXU_DTYPE)


def kernel(x_prompt, x_sample, cache_mla, cache_dsa_kv, cache_dsa_kidx, state_gla, state_lru_h, state_lru_conv, page_table, ln_gain, w_in, gla_w_g2, gla_b_g, gla_norm, lru_conv_w, lru_conv_b, lru_w_r, lru_b_r, lru_w_i, lru_b_i, lru_lambda, mla_q_norm, mla_w_uq, mla_kv_norm, mla_w_uk, mla_w_uv, w_branch, w_out, final_gain):
    B, S, D = x_prompt.shape
    DB, T, _ = x_sample.shape
    depth = w_in.shape[0]
    n_pages = page_table.shape[1]
    past_len = n_pages * PAGE_SIZE
    topk_p = min(DSA_TOPK, S // 4)
    topk_s = min(DSA_TOPK, (past_len + T) // 4)
    assert D == D_MODEL and T <= SUB and S % 256 == 0 and S >= 512

    cos_p, sin_p = _rope_tables(jnp.arange(S))
    cos_s, sin_s = _rope_tables(past_len + jnp.arange(SUB))
    seqs_per_tile = min(DB * SUB, 256) // SUB
    cos_s = jnp.tile(cos_s, (seqs_per_tile, 1))
    sin_s = jnp.tile(sin_s, (seqs_per_tile, 1))
    cache_kv = cache_dsa_kv.reshape(cache_dsa_kv.shape[:3] + (2 * DSA_HD,))

    xp = x_prompt.reshape(B * S, D)
    xs = jnp.pad(x_sample, ((0, 0), (0, SUB - T), (0, 0))).reshape(DB * SUB, D)
    fg = final_gain.reshape(1, D)
    outs = [[] for _ in range(12)]
    for l in range(depth):
        final = l == depth - 1
        w_l = _layout_w_in(w_in[l])
        gain = ln_gain[l].reshape(1, D)
        wg2 = jnp.pad(gla_w_g2[l], ((0, LANE - GLA_GATE_RANK), (0, 0))).astype(MXU_DTYPE)
        bg = gla_b_g[l].reshape(1, -1)
        gn = gla_norm[l].reshape(1, -1)
        cw = lru_conv_w[l]
        cb = lru_conv_b[l].reshape(1, -1)
        wr = _block_diag(lru_w_r[l])
        wi = _block_diag(lru_w_i[l])
        br = lru_b_r[l].reshape(1, -1)
        bi = lru_b_i[l].reshape(1, -1)
        lam = lru_lambda[l].reshape(1, -1)
        qn = mla_q_norm[l].reshape(1, -1)
        kvn = mla_kv_norm[l].reshape(1, -1)
        wuq3 = mla_w_uq[l].reshape(MLA_Q_LORA, MLA_HEADS, MLA_NOPE + MLA_ROPE)
        wuq = jnp.concatenate([wuq3[:, :, :MLA_NOPE].reshape(MLA_Q_LORA, -1),
                               wuq3[:, :, MLA_NOPE:].reshape(MLA_Q_LORA, -1)], axis=1).astype(MXU_DTYPE)
        wuk = mla_w_uk[l].reshape(MLA_KV_LORA, MLA_HEADS * MLA_NOPE).T.astype(MXU_DTYPE)
        wuv = jnp.transpose(mla_w_uv[l], (1, 0, 2)).astype(MXU_DTYPE)
        wb = w_branch[l].astype(MXU_DTYPE)
        wo = w_out[l].astype(MXU_DTYPE)

        p = _inproj(xp, gain, w_l)
        rows, mk, mq, kvf, kvb, kif, kib, dqb, dqib = _prep(p, cos_p, sin_p, qn, wuq, kvn, wuk)
        o_a, gla_s = _gla_prompt(p, B, S, wg2, bg, gn)
        o_b, lru_h, conv_new = _lru_prompt(p, B, S, cw, cb, wr, br, wi, bi, lam)
        o_c = _mla_prompt(mq, mk, p, B, S, wuv)
        o_d = _dsa_prompt(dqb, dqib, kib, kvb, p, B, S, topk_p)
        xp = _merge(xp, o_a, o_b, o_c, o_d, p, wb, wo, fg, final)
        outs[0].append(rows.reshape(B, S, MLA_ROW))
        outs[2].append(kvf.reshape(B, S, 2, DSA_KV_HEADS, DSA_HD))
        outs[4].append(kif.reshape(B, S, DSA_IDX_DIM))
        outs[6].append(gla_s)
        outs[8].append(lru_h.reshape(B, LRU_W))
        outs[10].append(conv_new)

        p = _inproj(xs, gain, w_l)
        rows, mk, mq, kvf, kvb, kif, kib, dqb, dqib = _prep(p, cos_s, sin_s, qn, wuq, kvn, wuk)
        p2 = p.reshape(DB, SUB * W_PAD)
        p3 = p.reshape(DB, SUB, W_PAD)
        o_a, gla_s = _gla_sample(p2, state_gla[l], T, wg2, bg, gn)
        o_b, lru_h, conv_new = _lru_sample(p2, state_lru_conv[l].reshape(DB, -1), state_lru_h[l], T,
                                           cw, cb, wr, br, wi, bi, lam)
        o_c = _mla_sample(page_table, cache_mla, l, mq.reshape(DB, SUB, -1), mk.reshape(DB, SUB, -1), p3, wuv)
        o_d = _dsa_sample(page_table, cache_dsa_kidx, cache_kv, l, dqb.reshape(DB, SUB, -1),
                          dqib.reshape(DB, SUB, -1), p3, topk_s)
        xs = _merge(xs, o_a.reshape(DB * SUB, -1), o_b.reshape(DB * SUB, -1), o_c.reshape(DB * SUB, -1),
                    o_d.reshape(DB * SUB, -1), p, wb, wo, fg, final)
        outs[1].append(rows.reshape(DB, SUB, MLA_ROW)[:, :T])
        outs[3].append(kvf.reshape(DB, SUB, 2, DSA_KV_HEADS, DSA_HD)[:, :T])
        outs[5].append(kif.reshape(DB, SUB, DSA_IDX_DIM)[:, :T])
        outs[7].append(gla_s)
        outs[9].append(lru_h)
        outs[11].append(conv_new.reshape(DB, CONV_W - 1, LRU_W))

    y_prompt = xp.reshape(B, S, D)
    y_sample = xs.reshape(DB, SUB, D)[:, :T]
    return (y_prompt, y_sample) + tuple(jnp.stack(o) for o in outs)
```

```python
import functools
import math

import numpy as np
import jax
import jax.numpy as jnp
from jax import lax
from jax.experimental import pallas as pl
from jax.experimental.pallas import tpu as pltpu

f32 = jnp.float32
bf16 = jnp.bfloat16
MXU_DTYPE = bf16

LANE = 128
SUB = 8
VMEM_LIMIT = 56 * 1024 * 1024
PAGES_PER_STEP = 16

D_MODEL = 1024
PAGE_SIZE = 128
N_BRANCH = 4
BRANCH_W = D_MODEL // 2
GLA_HEADS = 4
GLA_DV = BRANCH_W // GLA_HEADS
GLA_DK = GLA_DV // 2
GLA_GATE_RANK = 16
GLA_TAU = 16.0
GLA_CHUNK = 64
LRU_W = BRANCH_W
LRU_BLOCKS = 8
LRU_BW = LRU_W // LRU_BLOCKS
LRU_C = 8.0
CONV_W = 4
MLA_HEADS = 4
MLA_NOPE = 64
MLA_ROPE = 32
MLA_V = BRANCH_W // MLA_HEADS
MLA_Q_LORA = D_MODEL // 4
MLA_KV_LORA = D_MODEL // 4
MLA_ROW = MLA_KV_LORA + MLA_ROPE
MLA_SCALE = (MLA_NOPE + MLA_ROPE) ** -0.5
MLA_KPAD = MLA_KV_LORA + LANE
ROPE_THETA = 10000.0
DSA_HEADS = 4
DSA_KV_HEADS = 1
DSA_HD = BRANCH_W // DSA_HEADS
DSA_IDX_HEADS = 4
DSA_IDX_DIM = 64
DSA_TOPK = 256
EPS = 1e-6
IN_SIZES = (
    GLA_HEADS * GLA_DK, GLA_HEADS * GLA_DK, GLA_HEADS * GLA_DV, GLA_GATE_RANK, BRANCH_W,
    LRU_W, LRU_W,
    MLA_Q_LORA, MLA_KV_LORA, MLA_ROPE, BRANCH_W,
    DSA_HEADS * DSA_HD, DSA_KV_HEADS * DSA_HD, DSA_KV_HEADS * DSA_HD,
    DSA_IDX_HEADS * DSA_IDX_DIM, DSA_IDX_DIM, DSA_IDX_HEADS, BRANCH_W,
    N_BRANCH * D_MODEL,
)
(_GQ, _GK, _GV, _GLR, _GGATE, _LX, _LGATE, _CQ, _CKV, _KR, _CGATE,
 _DQ, _DK, _DV, _DQI, _DKI, _DWI, _DGATE, _MERGE) = range(len(IN_SIZES))
_IN_OFFS = np.concatenate([[0], np.cumsum(IN_SIZES)]).tolist()

OFF_MERGE = 0
OFF_GV = 4096
OFF_GGATE = 4608
OFF_LX = 5120
OFF_LGATE = 5632
OFF_CGATE = 6144
OFF_DQ = 6656
OFF_DGATE = 7168
OFF_DQI = 7680
OFF_GQ = 8192
OFF_GK = 8448
OFF_CQ = 8704
OFF_CKV = 8960
OFF_DKV = 9216
OFF_GLR = 9472
OFF_KR = 9600
OFF_DKI = 9728
W_PAD = 10240

KEY_NEG_INF = -2139095041
INT_MIN = -2147483648

NT_DIMS = (((1,), (1,)), ((), ()))
TN_DIMS = (((0,), (0,)), ((), ()))


def _cparams(*sem):
    return pltpu.CompilerParams(dimension_semantics=sem, vmem_limit_bytes=VMEM_LIMIT)


def _mm(a, b):
    return jnp.dot(a.astype(MXU_DTYPE), b.astype(MXU_DTYPE), preferred_element_type=f32)


def _mm_nt(a, b):
    return lax.dot_general(a.astype(MXU_DTYPE), b.astype(MXU_DTYPE), NT_DIMS, preferred_element_type=f32)


def _mm_tn(a, b):
    return lax.dot_general(a.astype(MXU_DTYPE), b.astype(MXU_DTYPE), TN_DIMS, preferred_element_type=f32)


def _rms(x, g):
    return x * lax.rsqrt(jnp.mean(x * x, axis=-1, keepdims=True) + EPS) * g


def _silu(x):
    return x * jax.nn.sigmoid(x)


def _softplus(x):
    return jnp.maximum(x, 0.0) + jnp.log1p(jnp.exp(-jnp.abs(x)))


def _neg_expm1(y):
    acc = jnp.full_like(y, 1.0 / 479001600.0)
    for k in (39916800.0, 3628800.0, 362880.0, 40320.0, 5040.0, 720.0, 120.0, 24.0, 6.0, 2.0, 1.0):
        acc = acc * y + 1.0 / k
    return jnp.where(y > -0.25, -(acc * y), 1.0 - jnp.exp(y))


def _sort_key(x):
    b = lax.bitcast_convert_type(x, jnp.int32)
    return b ^ ((b >> 31) & jnp.int32(0x7FFFFFFF))


def _softmax_update(s, v, m_ref, l_ref, acc_ref, v_transposed=False):
    m_old = m_ref[...]
    m_new = jnp.maximum(m_old, jnp.max(s, axis=1, keepdims=True))
    m_safe = jnp.where(m_new == -jnp.inf, 0.0, m_new)
    alpha = jnp.exp(m_old - m_safe)
    p = jnp.exp(s - m_safe)
    l_ref[...] = alpha * l_ref[...] + jnp.sum(p, axis=1, keepdims=True)
    acc_ref[...] = alpha * acc_ref[...] + (_mm_nt(p, v) if v_transposed else _mm(p, v))
    m_ref[...] = m_new


def _softmax_init(m_ref, l_ref, acc_ref):
    m_ref[...] = jnp.full(m_ref.shape, -jnp.inf, f32)
    l_ref[...] = jnp.zeros(l_ref.shape, f32)
    acc_ref[...] = jnp.zeros(acc_ref.shape, f32)


def _stack_heads(x, n, w):
    return jnp.concatenate([x[:, h * w:(h + 1) * w] for h in range(n)], axis=0)


def _inproj_kernel(x_ref, g_ref, w_ref, o_ref, h_ref):
    @pl.when(pl.program_id(1) == 0)
    def _():
        h_ref[...] = _rms(x_ref[...], g_ref[...]).astype(h_ref.dtype)

    o_ref[...] = jnp.dot(h_ref[...], w_ref[...], preferred_element_type=f32)


def _inproj(x, gain, w):
    n, d = x.shape
    tm = min(n, 1024)
    tn = 512
    return pl.pallas_call(
        _inproj_kernel,
        grid=(n // tm, W_PAD // tn),
        in_specs=[pl.BlockSpec((tm, d), lambda i, j: (i, 0)),
                  pl.BlockSpec((1, d), lambda i, j: (0, 0)),
                  pl.BlockSpec((d, tn), lambda i, j: (0, j))],
        out_specs=pl.BlockSpec((tm, tn), lambda i, j: (i, j)),
        out_shape=jax.ShapeDtypeStruct((n, W_PAD), f32),
        scratch_shapes=[pltpu.VMEM((tm, d), MXU_DTYPE)],
        compiler_params=_cparams("parallel", "arbitrary"),
        name="inproj",
    )(x, gain, w)


def _prep_kernel(cq_ref, ckv_ref, kr_ref, dq_ref, dqi_ref, dkv_ref, dki_ref, cos_ref, sin_ref,
                 qn_ref, wuq_ref, kvn_ref, wuk_ref,
                 rows_ref, mk_ref, mq_ref, kvf_ref, kvb_ref, kif_ref, kib_ref, dqb_ref, dqib_ref):
    tm = cq_ref.shape[0]
    cos = cos_ref[...]
    sin = sin_ref[...]
    lane = lax.broadcasted_iota(jnp.int32, (tm, LANE), 1)
    first_half = (lane % MLA_ROPE) < (MLA_ROPE // 2)

    def rope(x):
        swapped = jnp.where(first_half, pltpu.roll(x, LANE - MLA_ROPE // 2, 1), pltpu.roll(x, MLA_ROPE // 2, 1))
        return x * cos + swapped * sin

    c = _rms(ckv_ref[...], kvn_ref[...])
    kr = rope(kr_ref[...])
    rows_ref[:, 0:MLA_KV_LORA] = c
    rows_ref[:, MLA_KV_LORA:MLA_ROW] = kr[:, 0:MLA_ROPE]
    mk_ref[:, 0:MLA_KV_LORA] = c.astype(mk_ref.dtype)
    mk_ref[:, MLA_KV_LORA:MLA_KPAD] = kr.astype(mk_ref.dtype)

    q = _mm(_rms(cq_ref[...], qn_ref[...]), wuq_ref[...])
    q_nope = q[:, 0:MLA_HEADS * MLA_NOPE]
    q_rope = rope(q[:, MLA_HEADS * MLA_NOPE:])
    lane_q = lax.broadcasted_iota(jnp.int32, q_nope.shape, 1)
    wuk = wuk_ref[...]
    for h in range(MLA_HEADS):
        q_h = jnp.where(lane_q // MLA_NOPE == h, q_nope, 0.0)
        q_lat = _mm(q_h, wuk)
        base = h * MLA_KPAD
        mq_ref[:, base:base + MLA_KV_LORA] = (q_lat * MLA_SCALE).astype(mq_ref.dtype)
        qr = q_rope if h == 0 else pltpu.roll(q_rope, LANE - h * MLA_ROPE, 1)
        qr = jnp.where(lane < MLA_ROPE, qr, 0.0)
        mq_ref[:, base + MLA_KV_LORA:base + MLA_KPAD] = (qr * MLA_SCALE).astype(mq_ref.dtype)

    dkv = dkv_ref[...]
    kvf_ref[...] = dkv
    kvb_ref[...] = dkv.astype(kvb_ref.dtype)
    seg = dki_ref[...]
    kif_ref[...] = seg[:, 0:DSA_IDX_DIM]
    kib_ref[...] = jnp.where(lane < DSA_IDX_DIM, seg, 0.0).astype(kib_ref.dtype)
    dqb_ref[...] = (dq_ref[...] * (DSA_HD ** -0.5)).astype(dqb_ref.dtype)
    dqib_ref[...] = (dqi_ref[...] * (DSA_IDX_DIM ** -0.5)).astype(dqib_ref.dtype)


def _prep(p, cos_t, sin_t, q_norm, w_uq, kv_norm, w_uk):
    n = p.shape[0]
    tm = min(n, 256)
    nt = cos_t.shape[0] // tm

    def seg(off, w):
        return pl.BlockSpec((tm, w), lambda i: (i, off // w))

    def const(a):
        return pl.BlockSpec(a.shape, lambda i: (0,) * a.ndim)

    def out(w):
        return pl.BlockSpec((tm, w), lambda i: (i, 0))

    outs = [(MLA_ROW, f32), (MLA_KPAD, MXU_DTYPE), (MLA_HEADS * MLA_KPAD, MXU_DTYPE),
            (2 * DSA_HD, f32), (2 * DSA_HD, MXU_DTYPE), (DSA_IDX_DIM, f32), (LANE, MXU_DTYPE),
            (DSA_HEADS * DSA_HD, MXU_DTYPE), (DSA_IDX_HEADS * LANE, MXU_DTYPE)]
    return pl.pallas_call(
        _prep_kernel,
        grid=(n // tm,),
        in_specs=[seg(OFF_CQ, 256), seg(OFF_CKV, 256), seg(OFF_KR, LANE), seg(OFF_DQ, 512),
                  seg(OFF_DQI, 512), seg(OFF_DKV, 256), seg(OFF_DKI, LANE),
                  pl.BlockSpec((tm, LANE), lambda i: (i % nt, 0)),
                  pl.BlockSpec((tm, LANE), lambda i: (i % nt, 0)),
                  const(q_norm), const(w_uq), const(kv_norm), const(w_uk)],
        out_specs=[out(w) for w, _ in outs],
        out_shape=[jax.ShapeDtypeStruct((n, w), dt) for w, dt in outs],
        compiler_params=_cparams("parallel"),
        name="prep",
    )(p, p, p, p, p, p, p, cos_t, sin_t, q_norm, w_uq, kv_norm, w_uk)


def _gla_gate(glr, wg2, bg):
    z = _mm(glr, wg2) + bg
    return (jnp.minimum(z, 0.0) - jnp.log1p(jnp.exp(-jnp.abs(z)))) * (1.0 / GLA_TAU)


def _gla_out(o, gate, gn):
    return _rms(o, gn) * _silu(gate)


def _gla_prompt_kernel(q_ref, k_ref, v_ref, gate_ref, glr_ref, wg2_ref, bg_ref, gn_ref,
                       o_ref, s_ref, st_ref, *, n_chunks):
    t = pl.program_id(1)

    @pl.when(t == 0)
    def _():
        st_ref[...] = jnp.zeros(st_ref.shape, f32)

    C = GLA_CHUNK
    HK = GLA_HEADS * GLA_DK
    r_i = lax.broadcasted_iota(jnp.int32, (C, C), 0)
    c_i = lax.broadcasted_iota(jnp.int32, (C, C), 1)
    causal = r_i >= c_i
    tril = causal.astype(f32)
    lane_head = lax.broadcasted_iota(jnp.int32, (C, HK), 1) // GLA_DK
    lane_head_st = lax.broadcasted_iota(jnp.int32, (GLA_DV, HK), 1) // GLA_DK
    causal_h = jnp.concatenate([causal] * GLA_HEADS, axis=0)
    gn = gn_ref[...]
    wg2 = wg2_ref[...]
    bg = bg_ref[...]
    for c in range(n_chunks):
        rows = slice(c * C, (c + 1) * C)
        g = _gla_gate(glr_ref[rows, :], wg2, bg)
        b = jnp.dot(tril, g, precision=lax.Precision.HIGHEST, preferred_element_type=f32)
        b_last = b[C - 1:C, :]
        q = q_ref[rows, :] * (GLA_DK ** -0.5)
        k = k_ref[rows, :]
        v = v_ref[rows, :]
        q_dec = q * jnp.exp(b)
        k_dec = k * jnp.exp(-b)
        k_tail = k * jnp.exp(b_last - b)
        decay = jnp.exp(b_last)
        q_heads = jnp.concatenate(
            [jnp.where(lane_head == h, q_dec, 0.0) for h in range(GLA_HEADS)], axis=0)
        att = jnp.where(causal_h, _mm_nt(q_heads, k_dec), 0.0)
        st = st_ref[...]
        o_inter = _mm_nt(q_heads, st)
        gate = gate_ref[rows, :]
        for h in range(GLA_HEADS):
            hv = slice(h * GLA_DV, (h + 1) * GLA_DV)
            o_h = _mm(att[h * C:(h + 1) * C, :], v[:, hv]) + o_inter[h * C:(h + 1) * C, :]
            o_ref[rows, hv] = _gla_out(o_h, gate[:, hv], gn)
        kv = _mm_tn(v, k_tail)
        upd = jnp.zeros((GLA_DV, HK), f32)
        for h in range(GLA_HEADS):
            upd = upd + jnp.where(lane_head_st == h, kv[h * GLA_DV:(h + 1) * GLA_DV, :], 0.0)
        st_ref[...] = decay * st + upd

    @pl.when(t == pl.num_programs(1) - 1)
    def _():
        s_ref[...] = st_ref[...].T.reshape(GLA_HEADS, GLA_DK, GLA_DV)


def _gla_prompt(p, B, L, wg2, bg, gn):
    tl = min(L, 256)
    nt = L // tl

    def seg(off, w):
        return pl.BlockSpec((tl, w), lambda b, t: (b * nt + t, off // w))

    def const(a):
        return pl.BlockSpec(a.shape, lambda b, t: (0,) * a.ndim)

    return pl.pallas_call(
        functools.partial(_gla_prompt_kernel, n_chunks=tl // GLA_CHUNK),
        grid=(B, nt),
        in_specs=[seg(OFF_GQ, 256), seg(OFF_GK, 256), seg(OFF_GV, 512), seg(OFF_GGATE, 512),
                  seg(OFF_GLR, LANE), const(wg2), const(bg), const(gn)],
        out_specs=[pl.BlockSpec((tl, BRANCH_W), lambda b, t: (b * nt + t, 0)),
                   pl.BlockSpec((None, GLA_HEADS, GLA_DK, GLA_DV), lambda b, t: (b, 0, 0, 0))],
        out_shape=[jax.ShapeDtypeStruct((B * L, BRANCH_W), f32),
                   jax.ShapeDtypeStruct((B, GLA_HEADS, GLA_DK, GLA_DV), f32)],
        scratch_shapes=[pltpu.VMEM((GLA_DV, GLA_HEADS * GLA_DK), f32)],
        compiler_params=_cparams("parallel", "arbitrary"),
        name="gla_prompt",
    )(p, p, p, p, p, wg2, bg, gn)


def _gla_sample_kernel(*refs, n_tok, bg_rows):
    q_refs = refs[0:n_tok]
    k_refs = refs[n_tok:2 * n_tok]
    v_refs = refs[2 * n_tok:3 * n_tok]
    gate_refs = refs[3 * n_tok:4 * n_tok]
    glr_refs = refs[4 * n_tok:5 * n_tok]
    s0_ref, wg2_ref, bg_ref, gn_ref, o_ref, s_ref, tr_ref, oraw_ref = refs[5 * n_tok:]
    HK = GLA_HEADS * GLA_DK
    zpad = jnp.zeros((LANE - bg_rows, HK), f32)

    def col_major(x):
        return jnp.concatenate([x, zpad], axis=0).T

    for t in range(n_tok):
        g = _gla_gate(glr_refs[t][...], wg2_ref[...], bg_ref[...])
        tr_ref[3 * t + 0] = col_major(q_refs[t][...] * (GLA_DK ** -0.5))
        tr_ref[3 * t + 1] = col_major(k_refs[t][...])
        tr_ref[3 * t + 2] = col_major(jnp.exp(g))
    for i in range(bg_rows):
        s = s0_ref[i].reshape(HK, GLA_DV)
        for t in range(n_tok):
            qc = tr_ref[3 * t + 0, :, i:i + 1]
            kc = tr_ref[3 * t + 1, :, i:i + 1]
            ec = tr_ref[3 * t + 2, :, i:i + 1]
            vrow = v_refs[t][i:i + 1, :]
            vb = jnp.concatenate(
                [jnp.broadcast_to(vrow[:, h * GLA_DV:(h + 1) * GLA_DV], (GLA_DK, GLA_DV))
                 for h in range(GLA_HEADS)], axis=0)
            s = ec * s + kc * vb
            prod = qc * s
            oraw_ref[t, i:i + 1, :] = jnp.concatenate(
                [jnp.sum(prod[h * GLA_DK:(h + 1) * GLA_DK, :], axis=0, keepdims=True)
                 for h in range(GLA_HEADS)], axis=1)
        s_ref[i] = s.reshape(GLA_HEADS, GLA_DK, GLA_DV)
    gn = gn_ref[...]
    for t in range(n_tok):
        o = oraw_ref[t]
        gate = gate_refs[t][...]
        for h in range(GLA_HEADS):
            hv = slice(h * GLA_DV, (h + 1) * GLA_DV)
            o_ref[:, t * BRANCH_W + h * GLA_DV:t * BRANCH_W + (h + 1) * GLA_DV] = _gla_out(o[:, hv], gate[:, hv], gn)
    o_ref[:, n_tok * BRANCH_W:] = jnp.zeros((bg_rows, (SUB - n_tok) * BRANCH_W), f32)


def _gla_sample(p2, s0, n_tok, wg2, bg, gn):
    db = p2.shape[0]
    bgr = min(db, 16)

    def seg(t, off, w):
        return pl.BlockSpec((bgr, w), lambda i: (i, (t * W_PAD + off) // w))

    def const(a):
        return pl.BlockSpec(a.shape, lambda i: (0,) * a.ndim)

    in_specs = ([seg(t, OFF_GQ, 256) for t in range(n_tok)] + [seg(t, OFF_GK, 256) for t in range(n_tok)]
                + [seg(t, OFF_GV, 512) for t in range(n_tok)] + [seg(t, OFF_GGATE, 512) for t in range(n_tok)]
                + [seg(t, OFF_GLR, LANE) for t in range(n_tok)]
                + [pl.BlockSpec((bgr, GLA_HEADS, GLA_DK, GLA_DV), lambda i: (i, 0, 0, 0)),
                   const(wg2), const(bg), const(gn)])
    return pl.pallas_call(
        functools.partial(_gla_sample_kernel, n_tok=n_tok, bg_rows=bgr),
        grid=(db // bgr,),
        in_specs=in_specs,
        out_specs=[pl.BlockSpec((bgr, SUB * BRANCH_W), lambda i: (i, 0)),
                   pl.BlockSpec((bgr, GLA_HEADS, GLA_DK, GLA_DV), lambda i: (i, 0, 0, 0))],
        out_shape=[jax.ShapeDtypeStruct((db, SUB * BRANCH_W), f32),
                   jax.ShapeDtypeStruct(s0.shape, f32)],
        scratch_shapes=[pltpu.VMEM((3 * n_tok, GLA_HEADS * GLA_DK, LANE), f32),
                        pltpu.VMEM((n_tok, bgr, BRANCH_W), f32)],
        compiler_params=_cparams("parallel"),
        name="gla_sample",
    )(*([p2] * (5 * n_tok)), s0, wg2, bg, gn)


def _lru_gates(xc, wr, br, wi, bi, sp):
    r = jax.nn.sigmoid(_mm(xc, wr) + br)
    ig = jax.nn.sigmoid(_mm(xc, wi) + bi)
    log_a = -LRU_C * r * sp
    return jnp.exp(log_a), jnp.sqrt(_neg_expm1(2.0 * log_a)) * (ig * xc)


def _lru_prompt_kernel(x_ref, gate_ref, cw_ref, cb_ref, wr_ref, br_ref, wi_ref, bi_ref, lam_ref,
                       o_ref, hfin_ref, conv_ref, tail_ref, h_ref, a_ref, u_ref, hs_ref):
    t = pl.program_id(1)
    tl = x_ref.shape[0]

    @pl.when(t == 0)
    def _():
        tail_ref[...] = jnp.zeros(tail_ref.shape, f32)
        h_ref[...] = jnp.zeros(h_ref.shape, f32)

    x = x_ref[...]
    ext = jnp.concatenate([tail_ref[...], x], axis=0)
    cw = cw_ref[...]
    xc = jnp.zeros(x.shape, f32)
    for j in range(CONV_W):
        s = SUB - (CONV_W - 1) + j
        xc = xc + ext[s:s + tl, :] * cw[j:j + 1, :]
    xc = cb_ref[...] + xc
    a, u = _lru_gates(xc, wr_ref[...], br_ref[...], wi_ref[...], bi_ref[...], _softplus(-lam_ref[...]))
    a_ref[...] = a
    u_ref[...] = u

    def step(i, h):
        h = a_ref[pl.ds(i, 1), :] * h + u_ref[pl.ds(i, 1), :]
        hs_ref[pl.ds(i, 1), :] = h
        return h

    h = lax.fori_loop(0, tl, step, h_ref[...], unroll=8)
    h_ref[...] = h
    tail_ref[...] = x[tl - SUB:, :]
    o_ref[...] = hs_ref[...] * _silu(gate_ref[...])

    @pl.when(t == pl.num_programs(1) - 1)
    def _():
        hfin_ref[...] = h
        conv_ref[...] = x[tl - (CONV_W - 1):, :]


def _lru_prompt(p, B, L, cw, cb, wr, br, wi, bi, lam):
    tl = min(L, 256)
    nt = L // tl

    def seg(off, w):
        return pl.BlockSpec((tl, w), lambda b, t: (b * nt + t, off // w))

    def const(a):
        return pl.BlockSpec(a.shape, lambda b, t: (0,) * a.ndim)

    return pl.pallas_call(
        _lru_prompt_kernel,
        grid=(B, nt),
        in_specs=[seg(OFF_LX, 512), seg(OFF_LGATE, 512), const(cw), const(cb), const(wr), const(br),
                  const(wi), const(bi), const(lam)],
        out_specs=[pl.BlockSpec((tl, LRU_W), lambda b, t: (b * nt + t, 0)),
                   pl.BlockSpec((None, 1, LRU_W), lambda b, t: (b, 0, 0)),
                   pl.BlockSpec((None, CONV_W - 1, LRU_W), lambda b, t: (b, 0, 0))],
        out_shape=[jax.ShapeDtypeStruct((B * L, LRU_W), f32),
                   jax.ShapeDtypeStruct((B, 1, LRU_W), f32),
                   jax.ShapeDtypeStruct((B, CONV_W - 1, LRU_W), f32)],
        scratch_shapes=[pltpu.VMEM((SUB, LRU_W), f32), pltpu.VMEM((1, LRU_W), f32),
                        pltpu.VMEM((tl, LRU_W), f32), pltpu.VMEM((tl, LRU_W), f32),
                        pltpu.VMEM((tl, LRU_W), f32)],
        compiler_params=_cparams("parallel", "arbitrary"),
        name="lru_prompt",
    )(p, p, cw, cb, wr, br, wi, bi, lam)


def _lru_sample_kernel(*refs, n_tok):
    x_refs = refs[0:n_tok]
    gate_refs = refs[n_tok:2 * n_tok]
    (buf_ref, h0_ref, cw_ref, cb_ref, wr_ref, br_ref, wi_ref, bi_ref, lam_ref,
     o_ref, hfin_ref, conv_ref) = refs[2 * n_tok:]
    W = LRU_W
    xs = [buf_ref[:, j * W:(j + 1) * W] for j in range(CONV_W - 1)] + [r[...] for r in x_refs]
    cw = cw_ref[...]
    sp = _softplus(-lam_ref[...])
    h = h0_ref[...]
    for t in range(n_tok):
        xc = jnp.zeros(h.shape, f32)
        for j in range(CONV_W):
            xc = xc + xs[t + j] * cw[j:j + 1, :]
        xc = cb_ref[...] + xc
        a, u = _lru_gates(xc, wr_ref[...], br_ref[...], wi_ref[...], bi_ref[...], sp)
        h = a * h + u
        o_ref[:, t * W:(t + 1) * W] = h * _silu(gate_refs[t][...])
    o_ref[:, n_tok * W:] = jnp.zeros((h.shape[0], (SUB - n_tok) * W), f32)
    hfin_ref[...] = h
    for j in range(CONV_W - 1):
        conv_ref[:, j * W:(j + 1) * W] = xs[n_tok + j]


def _lru_sample(p2, buf, h0, n_tok, cw, cb, wr, br, wi, bi, lam):
    db = p2.shape[0]

    def seg(t, off, w):
        return pl.BlockSpec((db, w), lambda i: (0, (t * W_PAD + off) // w))

    def const(a):
        return pl.BlockSpec(a.shape, lambda i: (0,) * a.ndim)

    in_specs = ([seg(t, OFF_LX, 512) for t in range(n_tok)] + [seg(t, OFF_LGATE, 512) for t in range(n_tok)]
                + [const(a) for a in (buf, h0, cw, cb, wr, br, wi, bi, lam)])
    return pl.pallas_call(
        functools.partial(_lru_sample_kernel, n_tok=n_tok),
        grid=(1,),
        in_specs=in_specs,
        out_specs=[pl.BlockSpec((db, SUB * LRU_W), lambda i: (0, 0)),
                   pl.BlockSpec((db, LRU_W), lambda i: (0, 0)),
                   pl.BlockSpec((db, (CONV_W - 1) * LRU_W), lambda i: (0, 0))],
        out_shape=[jax.ShapeDtypeStruct((db, SUB * LRU_W), f32),
                   jax.ShapeDtypeStruct((db, LRU_W), f32),
                   jax.ShapeDtypeStruct((db, (CONV_W - 1) * LRU_W), f32)],
        compiler_params=_cparams("arbitrary"),
        name="lru_sample",
    )(*([p2] * (2 * n_tok)), buf, h0, cw, cb, wr, br, wi, bi, lam)


def _mla_finish(acc_ref, l_ref, wuv_ref, gate, o_ref, rows):
    o = acc_ref[...] / l_ref[...]
    for h in range(MLA_HEADS):
        oc = _mm(o[h * rows:(h + 1) * rows, :], wuv_ref[h])
        o_ref[:, h * MLA_V:(h + 1) * MLA_V] = oc * _silu(gate[:, h * MLA_V:(h + 1) * MLA_V])


def _mla_prompt_kernel(q_ref, k_ref, wuv_ref, gate_ref, o_ref, m_ref, l_ref, acc_ref, *, tq, tk):
    i = pl.program_id(1)
    qs = _stack_heads(q_ref[...], MLA_HEADS, MLA_KPAD)
    _softmax_init(m_ref, l_ref, acc_ref)
    n_full = (i * tq) // tk

    def body(j, carry):
        k = k_ref[pl.ds(pl.multiple_of(j * tk, tk), tk), :]
        _softmax_update(_mm_nt(qs, k), k[:, 0:MLA_KV_LORA], m_ref, l_ref, acc_ref)
        return carry

    lax.fori_loop(0, n_full, body, 0)
    k = k_ref[pl.ds(pl.multiple_of(n_full * tk, tk), tk), :]
    s = _mm_nt(qs, k)
    row = i * tq + lax.broadcasted_iota(jnp.int32, s.shape, 0) % tq
    col = n_full * tk + lax.broadcasted_iota(jnp.int32, s.shape, 1)
    _softmax_update(jnp.where(col <= row, s, -jnp.inf), k[:, 0:MLA_KV_LORA], m_ref, l_ref, acc_ref)
    _mla_finish(acc_ref, l_ref, wuv_ref, gate_ref[...], o_ref, tq)


def _mla_prompt(mq, mk, p, B, L, wuv):
    tq = 128
    tk = min(L, 512)
    nq = L // tq
    rows = MLA_HEADS * tq
    return pl.pallas_call(
        functools.partial(_mla_prompt_kernel, tq=tq, tk=tk),
        grid=(B, nq),
        in_specs=[pl.BlockSpec((tq, MLA_HEADS * MLA_KPAD), lambda b, i: (b * nq + i, 0)),
                  pl.BlockSpec((L, MLA_KPAD), lambda b, i: (b, 0)),
                  pl.BlockSpec(wuv.shape, lambda b, i: (0, 0, 0)),
                  pl.BlockSpec((tq, BRANCH_W), lambda b, i: (b * nq + i, OFF_CGATE // BRANCH_W))],
        out_specs=pl.BlockSpec((tq, BRANCH_W), lambda b, i: (b * nq + i, 0)),
        out_shape=jax.ShapeDtypeStruct((B * L, BRANCH_W), f32),
        scratch_shapes=[pltpu.VMEM((rows, 1), f32), pltpu.VMEM((rows, 1), f32),
                        pltpu.VMEM((rows, MLA_KV_LORA), f32)],
        compiler_params=_cparams("parallel", "arbitrary"),
        name="mla_prompt",
    )(mq, mk, wuv, p)


def _new_block(x):
    return jnp.concatenate([x.astype(f32), jnp.zeros((LANE - SUB, x.shape[1]), f32)], axis=0)


def _mla_sample_kernel(pt_ref, q_ref, knew_ref, wuv_ref, gate_ref, *rest, n_grp):
    page_refs = rest[0:n_grp]
    o_ref, m_ref, l_ref, acc_ref = rest[n_grp:]
    j = pl.program_id(1)

    @pl.when(j == 0)
    def _():
        _softmax_init(m_ref, l_ref, acc_ref)

    qs = _stack_heads(q_ref[...], MLA_HEADS, MLA_KPAD)
    k_t = jnp.concatenate([r[...].astype(MXU_DTYPE) for r in page_refs], axis=1)
    k_t_pad = jnp.concatenate([k_t, jnp.zeros((MLA_KPAD - MLA_ROW, k_t.shape[1]), MXU_DTYPE)], axis=0)
    _softmax_update(_mm(qs, k_t_pad), k_t[0:MLA_KV_LORA, :], m_ref, l_ref, acc_ref, v_transposed=True)

    @pl.when(j == pl.num_programs(1) - 1)
    def _():
        kn = _new_block(knew_ref[...]).astype(MXU_DTYPE)
        s = _mm_nt(qs, kn)
        tok = lax.broadcasted_iota(jnp.int32, s.shape, 0) % SUB
        col = lax.broadcasted_iota(jnp.int32, s.shape, 1)
        _softmax_update(jnp.where(col <= tok, s, -jnp.inf), kn[:, 0:MLA_KV_LORA], m_ref, l_ref, acc_ref)
        _mla_finish(acc_ref, l_ref, wuv_ref, gate_ref[...], o_ref, SUB)


def _mla_sample(page_table, cache, layer, mq3, mk3, p3, wuv):
    db, n_pages = page_table.shape
    n_grp = math.gcd(n_pages, PAGES_PER_STEP)
    n_steps = n_pages // n_grp
    rows = MLA_HEADS * SUB

    def page_spec(g):
        return pl.BlockSpec((None, None, MLA_ROW, PAGE_SIZE),
                            lambda b, j, pt: (layer, pt[b * n_pages + j * n_grp + g], 0, 0))

    grid_spec = pltpu.PrefetchScalarGridSpec(
        num_scalar_prefetch=1,
        grid=(db, n_steps),
        in_specs=[pl.BlockSpec((None, SUB, MLA_HEADS * MLA_KPAD), lambda b, j, pt: (b, 0, 0)),
                  pl.BlockSpec((None, SUB, MLA_KPAD), lambda b, j, pt: (b, 0, 0)),
                  pl.BlockSpec(wuv.shape, lambda b, j, pt: (0, 0, 0)),
                  pl.BlockSpec((None, SUB, BRANCH_W), lambda b, j, pt: (b, 0, OFF_CGATE // BRANCH_W))]
                 + [page_spec(g) for g in range(n_grp)],
        out_specs=pl.BlockSpec((None, SUB, BRANCH_W), lambda b, j, pt: (b, 0, 0)),
        scratch_shapes=[pltpu.VMEM((rows, 1), f32), pltpu.VMEM((rows, 1), f32),
                        pltpu.VMEM((rows, MLA_KV_LORA), f32)])
    return pl.pallas_call(
        functools.partial(_mla_sample_kernel, n_grp=n_grp),
        grid_spec=grid_spec,
        out_shape=jax.ShapeDtypeStruct((db, SUB, BRANCH_W), f32),
        compiler_params=_cparams("parallel", "arbitrary"),
        name="mla_sample",
    )(page_table.reshape(-1), mq3, mk3, wuv, p3, *([cache] * n_grp))


def _index_scores(qi_stack, wcols, keys_b, rows, keys_transposed=False):
    sc = jnp.maximum(_mm(qi_stack, keys_b) if keys_transposed else _mm_nt(qi_stack, keys_b), 0.0)
    tot = sc[0:rows, :] * wcols[0]
    for h in range(1, DSA_IDX_HEADS):
        tot = tot + sc[h * rows:(h + 1) * rows, :] * wcols[h]
    return jnp.where(tot == 0.0, 0.0, tot)


def _wi_cols(seg):
    scale = DSA_IDX_HEADS ** -0.5
    return [seg[:, DSA_IDX_DIM + h:DSA_IDX_DIM + h + 1] * scale for h in range(DSA_IDX_HEADS)]


def _kth_largest(count_ge, rows, topk):
    t0 = jnp.where(count_ge(jnp.zeros((rows, 1), jnp.int32)) >= topk,
                   jnp.int32(0), jnp.int32(INT_MIN))

    def bit_body(it, t):
        cand = t + lax.shift_left(jnp.int32(1), jnp.int32(30) - it)
        return jnp.where(count_ge(cand) >= topk, cand, t)

    return lax.fori_loop(0, 31, bit_body, t0)


def _dsa_finish(acc_ref, l_ref, gate, o_ref, rows):
    o = acc_ref[...] / l_ref[...]
    for h in range(DSA_HEADS):
        hv = slice(h * DSA_HD, (h + 1) * DSA_HD)
        o_ref[:, hv] = o[h * rows:(h + 1) * rows, :] * _silu(gate[:, hv])


def _dsa_prompt_kernel(q_ref, qi_ref, wi_ref, gate_ref, kib_ref, kvb_ref, u_ref, o_ref,
                       keys_ref, m_ref, l_ref, acc_ref, *, tq, tk, topk):
    i = pl.program_id(1)
    n_ch = (i * tq) // tk + 1
    qi = _stack_heads(qi_ref[...], DSA_IDX_HEADS, LANE)
    wcols = _wi_cols(wi_ref[...])
    row_pos = i * tq + lax.broadcasted_iota(jnp.int32, (tq, tk), 0)
    col_in = lax.broadcasted_iota(jnp.int32, (tq, tk), 1)

    def score_body(j, carry):
        kb = kib_ref[pl.ds(pl.multiple_of(j * tk, tk), tk), :]
        tot = _index_scores(qi, wcols, kb, tq)
        keys_ref[j] = _sort_key(jnp.where(j * tk + col_in <= row_pos, tot, -jnp.inf))
        return carry

    lax.fori_loop(0, n_ch, score_body, 0)

    def count(pred, bound):
        bound_b = jnp.broadcast_to(bound, (tq, LANE))

        def body(j, acc):
            for c in range(tk // LANE):
                acc = acc + jnp.where(pred(keys_ref[j, :, c * LANE:(c + 1) * LANE], bound_b), 1.0, 0.0)
            return acc

        acc = lax.fori_loop(0, n_ch, body, jnp.zeros((tq, LANE), f32))
        return jnp.sum(acc, axis=1, keepdims=True)

    thr = _kth_largest(lambda cand: count(lambda kk, bb: kk >= bb, cand), tq, topk)
    need = topk - count(lambda kk, bb: kk > bb, thr)
    tie_ok = thr > KEY_NEG_INF
    qs = _stack_heads(q_ref[...], DSA_HEADS, DSA_HD)
    _softmax_init(m_ref, l_ref, acc_ref)

    def att_body(j, run):
        kk = keys_ref[j]
        eq = (kk == thr) & tie_ok
        eqf = eq.astype(f32)
        before = _mm(eqf, u_ref[...]) + run
        sel = (kk > thr) | (eq & (before < need))
        kv = kvb_ref[pl.ds(pl.multiple_of(j * tk, tk), tk), :]
        s = _mm_nt(qs, kv[:, 0:DSA_HD])
        s = jnp.where(jnp.concatenate([sel] * DSA_HEADS, axis=0), s, -jnp.inf)
        _softmax_update(s, kv[:, DSA_HD:], m_ref, l_ref, acc_ref)
        return run + jnp.sum(eqf, axis=1, keepdims=True)

    lax.fori_loop(0, n_ch, att_body, jnp.zeros((tq, 1), f32))
    _dsa_finish(acc_ref, l_ref, gate_ref[...], o_ref, tq)


def _dsa_prompt(dqb, dqib, kib, kvb, p, B, L, topk):
    tq = 128
    tk = min(L, 512)
    nq = L // tq
    rows = DSA_HEADS * tq
    upper = (np.arange(tk)[:, None] < np.arange(tk)[None, :]).astype(np.float32)
    u = jnp.asarray(upper, MXU_DTYPE)
    return pl.pallas_call(
        functools.partial(_dsa_prompt_kernel, tq=tq, tk=tk, topk=topk),
        grid=(B, nq),
        in_specs=[pl.BlockSpec((tq, DSA_HEADS * DSA_HD), lambda b, i: (b * nq + i, 0)),
                  pl.BlockSpec((tq, DSA_IDX_HEADS * LANE), lambda b, i: (b * nq + i, 0)),
                  pl.BlockSpec((tq, LANE), lambda b, i: (b * nq + i, OFF_DKI // LANE)),
                  pl.BlockSpec((tq, BRANCH_W), lambda b, i: (b * nq + i, OFF_DGATE // BRANCH_W)),
                  pl.BlockSpec((L, LANE), lambda b, i: (b, 0)),
                  pl.BlockSpec((L, 2 * DSA_HD), lambda b, i: (b, 0)),
                  pl.BlockSpec((tk, tk), lambda b, i: (0, 0))],
        out_specs=pl.BlockSpec((tq, BRANCH_W), lambda b, i: (b * nq + i, 0)),
        out_shape=jax.ShapeDtypeStruct((B * L, BRANCH_W), f32),
        scratch_shapes=[pltpu.VMEM((L // tk, tq, tk), jnp.int32),
                        pltpu.VMEM((rows, 1), f32), pltpu.VMEM((rows, 1), f32),
                        pltpu.VMEM((rows, DSA_HD), f32)],
        compiler_params=_cparams("parallel", "arbitrary"),
        name="dsa_prompt",
    )(dqb, dqib, p, p, kib, kvb, u)


def _lane_prefix(x):
    n = x.shape[1]
    lane = lax.broadcasted_iota(jnp.int32, x.shape, 1)
    s = 1
    while s < n:
        x = x + jnp.where(lane >= s, pltpu.roll(x, s, 1), 0.0)
        s *= 2
    return x


def _dsa_scores_kernel(pt_ref, qi_ref, seg_ref, *rest, n_grp):
    ki_refs = rest[0:n_grp]
    keys_ref, knew_ref = rest[n_grp:]
    j = pl.program_id(1)
    seg = seg_ref[...]
    wcols = _wi_cols(seg)
    qi = _stack_heads(qi_ref[...], DSA_IDX_HEADS, LANE)
    k_t = jnp.concatenate([r[...].astype(MXU_DTYPE) for r in ki_refs], axis=1)
    k_t = jnp.concatenate([k_t, jnp.zeros((LANE - DSA_IDX_DIM, k_t.shape[1]), MXU_DTYPE)], axis=0)
    keys = _sort_key(_index_scores(qi, wcols, k_t, SUB, keys_transposed=True))
    for g in range(n_grp):
        keys_ref[g] = keys[:, g * PAGE_SIZE:(g + 1) * PAGE_SIZE]

    @pl.when(j == pl.num_programs(1) - 1)
    def _():
        lane = lax.broadcasted_iota(jnp.int32, (LANE, LANE), 1)
        kn = jnp.where(lane < DSA_IDX_DIM, _new_block(seg), 0.0)
        tot = _index_scores(qi, wcols, kn, SUB)
        tok = lax.broadcasted_iota(jnp.int32, tot.shape, 0)
        col = lax.broadcasted_iota(jnp.int32, tot.shape, 1)
        knew_ref[...] = _sort_key(jnp.where(col <= tok, tot, -jnp.inf))


def _dsa_threshold_kernel(keys_ref, knew_ref, thr_ref, need_ref, *, topk):
    n_tiles, rb, _ = keys_ref.shape

    def count(pred, bound):
        bound_b = jnp.broadcast_to(bound, (rb, LANE))

        def body(c, acc):
            return acc + jnp.where(pred(keys_ref[c], bound_b), 1.0, 0.0)

        acc = lax.fori_loop(0, n_tiles, body, jnp.where(pred(knew_ref[...], bound_b), 1.0, 0.0),
                            unroll=math.gcd(n_tiles, 8))
        return jnp.sum(acc, axis=1, keepdims=True)

    thr = _kth_largest(lambda cand: count(lambda kk, bb: kk >= bb, cand), rb, topk)
    thr_ref[...] = thr
    need_ref[...] = topk - count(lambda kk, bb: kk > bb, thr)


def _dsa_attend_kernel(pt_ref, q_ref, kvnew_ref, gate_ref, keys_ref, knew_ref, thr_ref, need_ref, *rest, n_grp):
    kv_refs = rest[0:n_grp]
    o_ref, run_ref, m_ref, l_ref, acc_ref = rest[n_grp:]
    j = pl.program_id(1)

    @pl.when(j == 0)
    def _():
        run_ref[...] = jnp.zeros(run_ref.shape, f32)
        _softmax_init(m_ref, l_ref, acc_ref)

    def select(kk):
        thr = thr_ref[...]
        eq = (kk == thr) & (thr > KEY_NEG_INF)
        eqf = eq.astype(f32)
        incl = _lane_prefix(eqf)
        sel = (kk > thr) | (eq & (run_ref[...] + incl - eqf < need_ref[...]))
        run_ref[...] = run_ref[...] + incl[:, -1:]
        return jnp.concatenate([sel] * DSA_HEADS, axis=0)

    qs = _stack_heads(q_ref[...], DSA_HEADS, DSA_HD)
    sel = select(jnp.concatenate([keys_ref[g] for g in range(n_grp)], axis=1))
    k = jnp.concatenate([r[pl.ds(0, PAGE_SIZE, stride=2), :].astype(MXU_DTYPE) for r in kv_refs], axis=0)
    v = jnp.concatenate([r[pl.ds(1, PAGE_SIZE, stride=2), :].astype(MXU_DTYPE) for r in kv_refs], axis=0)
    _softmax_update(jnp.where(sel, _mm_nt(qs, k), -jnp.inf), v, m_ref, l_ref, acc_ref)

    @pl.when(j == pl.num_programs(1) - 1)
    def _():
        sel_n = select(knew_ref[...])
        kvn = _new_block(kvnew_ref[...]).astype(MXU_DTYPE)
        s_n = jnp.where(sel_n, _mm_nt(qs, kvn[:, 0:DSA_HD]), -jnp.inf)
        _softmax_update(s_n, kvn[:, DSA_HD:], m_ref, l_ref, acc_ref)
        _dsa_finish(acc_ref, l_ref, gate_ref[...], o_ref, SUB)


def _dsa_sample(page_table, cache_ki, cache_kv, layer, dqb3, dqib3, p3, topk):
    db, n_pages = page_table.shape
    n_grp = math.gcd(n_pages, PAGES_PER_STEP)
    n_steps = n_pages // n_grp
    n_rows = db * SUB
    rows = DSA_HEADS * SUB
    pt = page_table.reshape(-1)

    def page_spec(shape):
        def spec(g):
            return pl.BlockSpec((None, None) + shape,
                                lambda b, j, pt: (layer, pt[b * n_pages + j * n_grp + g], 0, 0))
        return [spec(g) for g in range(n_grp)]

    def row3(w, col):
        return pl.BlockSpec((None, SUB, w), lambda b, j, pt: (b, 0, col))

    keys_spec = pl.BlockSpec((n_grp, SUB, PAGE_SIZE), lambda b, j, pt: (j, b, 0))
    per_seq = lambda w: pl.BlockSpec((SUB, w), lambda b, j, pt: (b, 0))

    keys, knew = pl.pallas_call(
        functools.partial(_dsa_scores_kernel, n_grp=n_grp),
        grid_spec=pltpu.PrefetchScalarGridSpec(
            num_scalar_prefetch=1,
            grid=(db, n_steps),
            in_specs=[row3(DSA_IDX_HEADS * LANE, 0), row3(LANE, OFF_DKI // LANE)]
                     + page_spec((DSA_IDX_DIM, PAGE_SIZE)),
            out_specs=[keys_spec, per_seq(LANE)]),
        out_shape=[jax.ShapeDtypeStruct((n_pages, n_rows, PAGE_SIZE), jnp.int32),
                   jax.ShapeDtypeStruct((n_rows, LANE), jnp.int32)],
        compiler_params=_cparams("parallel", "arbitrary"),
        name="dsa_scores",
    )(pt, dqib3, p3, *([cache_ki] * n_grp))

    rb = min(n_rows, LANE)
    thr, need = pl.pallas_call(
        functools.partial(_dsa_threshold_kernel, topk=topk),
        grid=(n_rows // rb,),
        in_specs=[pl.BlockSpec((n_pages, rb, PAGE_SIZE), lambda i: (0, i, 0)),
                  pl.BlockSpec((rb, LANE), lambda i: (i, 0))],
        out_specs=[pl.BlockSpec((rb, 1), lambda i: (i, 0)), pl.BlockSpec((rb, 1), lambda i: (i, 0))],
        out_shape=[jax.ShapeDtypeStruct((n_rows, 1), jnp.int32), jax.ShapeDtypeStruct((n_rows, 1), f32)],
        compiler_params=_cparams("parallel"),
        name="dsa_threshold",
    )(keys, knew)

    return pl.pallas_call(
        functools.partial(_dsa_attend_kernel, n_grp=n_grp),
        grid_spec=pltpu.PrefetchScalarGridSpec(
            num_scalar_prefetch=1,
            grid=(db, n_steps),
            in_specs=[row3(DSA_HEADS * DSA_HD, 0), row3(2 * DSA_HD, OFF_DKV // (2 * DSA_HD)),
                      row3(BRANCH_W, OFF_DGATE // BRANCH_W), keys_spec, per_seq(LANE), per_seq(1), per_seq(1)]
                     + page_spec((2 * PAGE_SIZE, DSA_HD)),
            out_specs=pl.BlockSpec((None, SUB, BRANCH_W), lambda b, j, pt: (b, 0, 0)),
            scratch_shapes=[pltpu.VMEM((SUB, 1), f32),
                            pltpu.VMEM((rows, 1), f32), pltpu.VMEM((rows, 1), f32),
                            pltpu.VMEM((rows, DSA_HD), f32)]),
        out_shape=jax.ShapeDtypeStruct((db, SUB, BRANCH_W), f32),
        compiler_params=_cparams("parallel", "arbitrary"),
        name="dsa_attend",
    )(pt, dqb3, p3, p3, keys, knew, thr, need, *([cache_kv] * n_grp))


def _merge_kernel(x_ref, oa_ref, ob_ref, oc_ref, od_ref, mg_ref, wb_ref, wo_ref, fg_ref, y_ref, *, final):
    mixed = jnp.zeros(x_ref.shape, f32)
    for n, o_ref in enumerate((oa_ref, ob_ref, oc_ref, od_ref)):
        gate = jax.nn.sigmoid(mg_ref[:, n * D_MODEL:(n + 1) * D_MODEL])
        mixed = mixed + gate * _mm(o_ref[...], wb_ref[n])
    y = x_ref[...] + _mm(mixed, wo_ref[...])
    y_ref[...] = _rms(y, fg_ref[...]) if final else y


def _merge(x, oa, ob, oc, od, p, wb, wo, fg, final):
    n = x.shape[0]
    tm = min(n, 256)

    def row(w):
        return pl.BlockSpec((tm, w), lambda i: (i, 0))

    def const(a):
        return pl.BlockSpec(a.shape, lambda i: (0,) * a.ndim)

    return pl.pallas_call(
        functools.partial(_merge_kernel, final=final),
        grid=(n // tm,),
        in_specs=[row(D_MODEL), row(BRANCH_W), row(BRANCH_W), row(BRANCH_W), row(BRANCH_W),
                  pl.BlockSpec((tm, N_BRANCH * D_MODEL), lambda i: (i, OFF_MERGE)),
                  const(wb), const(wo), const(fg)],
        out_specs=row(D_MODEL),
        out_shape=jax.ShapeDtypeStruct((n, D_MODEL), f32),
        compiler_params=_cparams("parallel"),
        name="merge",
    )(x, oa, ob, oc, od, p, wb, wo, fg)


def _pad_cols(w, width):
    return jnp.pad(w, ((0, 0), (0, width - w.shape[1])))


def _layout_w_in(w):
    def col(i):
        return w[:, _IN_OFFS[i]:_IN_OFFS[i + 1]]

    dqi = col(_DQI)
    dqi_pad = jnp.concatenate(
        [_pad_cols(dqi[:, h * DSA_IDX_DIM:(h + 1) * DSA_IDX_DIM], LANE) for h in range(DSA_IDX_HEADS)], axis=1)
    parts = [col(_MERGE), col(_GV), col(_GGATE), col(_LX), col(_LGATE), col(_CGATE), col(_DQ), col(_DGATE),
             dqi_pad, col(_GQ), col(_GK), col(_CQ), col(_CKV), col(_DK), col(_DV),
             _pad_cols(col(_GLR), LANE), _pad_cols(col(_KR), LANE),
             _pad_cols(jnp.concatenate([col(_DKI), col(_DWI)], axis=1), LANE)]
    return _pad_cols(jnp.concatenate(parts, axis=1), W_PAD).astype(MXU_DTYPE)


def _rope_tables(pos):
    half = MLA_ROPE // 2
    inv = ROPE_THETA ** (-jnp.arange(half, dtype=f32) / half)
    ang = pos.astype(f32)[:, None] * inv[None, :]
    lane = np.arange(LANE)
    sign = np.where((lane % MLA_ROPE) < half, -1.0, 1.0).astype(np.float32)
    return jnp.tile(jnp.cos(ang), (1, LANE // half)), jnp.tile(jnp.sin(ang), (1, LANE // half)) * sign


def _block_diag(w):
    eye = jnp.eye(LRU_BLOCKS, dtype=w.dtype)
    return jnp.einsum('nde,nm->ndme', w, eye).reshape(LRU_W, LRU_W).astype(MXU_DTYPE)


def kernel(x_prompt, x_sample, cache_mla, cache_dsa_kv, cache_dsa_kidx, state_gla, state_lru_h, state_lru_conv, page_table, ln_gain, w_in, gla_w_g2, gla_b_g, gla_norm, lru_conv_w, lru_conv_b, lru_w_r, lru_b_r, lru_w_i, lru_b_i, lru_lambda, mla_q_norm, mla_w_uq, mla_kv_norm, mla_w_uk, mla_w_uv, w_branch, w_out, final_gain):
    B, S, D = x_prompt.shape
    DB, T, _ = x_sample.shape
    depth = w_in.shape[0]
    n_pages = page_table.shape[1]
    past_len = n_pages * PAGE_SIZE
    topk_p = min(DSA_TOPK, S // 4)
    topk_s = min(DSA_TOPK, (past_len + T) // 4)
    assert D == D_MODEL and T <= SUB and S % 256 == 0 and S >= 512

    cos_p, sin_p = _rope_tables(jnp.arange(S))
    cos_s, sin_s = _rope_tables(past_len + jnp.arange(SUB))
    seqs_per_tile = min(DB * SUB, 256) // SUB
    cos_s = jnp.tile(cos_s, (seqs_per_tile, 1))
    sin_s = jnp.tile(sin_s, (seqs_per_tile, 1))
    cache_mla_t = jnp.swapaxes(cache_mla, 2, 3)
    cache_ki_t = jnp.swapaxes(cache_dsa_kidx, 2, 3)
    cache_kv = cache_dsa_kv.reshape(cache_dsa_kv.shape[:2] + (2 * PAGE_SIZE, DSA_HD))

    xp = x_prompt.reshape(B * S, D)
    xs = jnp.pad(x_sample, ((0, 0), (0, SUB - T), (0, 0))).reshape(DB * SUB, D)
    fg = final_gain.reshape(1, D)
    outs = [[] for _ in range(12)]
    for l in range(depth):
        final = l == depth - 1
        w_l = _layout_w_in(w_in[l])
        gain = ln_gain[l].reshape(1, D)
        wg2 = jnp.pad(gla_w_g2[l], ((0, LANE - GLA_GATE_RANK), (0, 0))).astype(MXU_DTYPE)
        bg = gla_b_g[l].reshape(1, -1)
        gn = gla_norm[l].reshape(1, -1)
        cw = lru_conv_w[l]
        cb = lru_conv_b[l].reshape(1, -1)
        wr = _block_diag(lru_w_r[l])
        wi = _block_diag(lru_w_i[l])
        br = lru_b_r[l].reshape(1, -1)
        bi = lru_b_i[l].reshape(1, -1)
        lam = lru_lambda[l].reshape(1, -1)
        qn = mla_q_norm[l].reshape(1, -1)
        kvn = mla_kv_norm[l].reshape(1, -1)
        wuq3 = mla_w_uq[l].reshape(MLA_Q_LORA, MLA_HEADS, MLA_NOPE + MLA_ROPE)
        wuq = jnp.concatenate([wuq3[:, :, :MLA_NOPE].reshape(MLA_Q_LORA, -1),
                               wuq3[:, :, MLA_NOPE:].reshape(MLA_Q_LORA, -1)], axis=1).astype(MXU_DTYPE)
        wuk = mla_w_uk[l].reshape(MLA_KV_LORA, MLA_HEADS * MLA_NOPE).T.astype(MXU_DTYPE)
        wuv = jnp.transpose(mla_w_uv[l], (1, 0, 2)).astype(MXU_DTYPE)
        wb = w_branch[l].astype(MXU_DTYPE)
        wo = w_out[l].astype(MXU_DTYPE)

        p = _inproj(xp, gain, w_l)
        rows, mk, mq, kvf, kvb, kif, kib, dqb, dqib = _prep(p, cos_p, sin_p, qn, wuq, kvn, wuk)
        o_a, gla_s = _gla_prompt(p, B, S, wg2, bg, gn)
        o_b, lru_h, conv_new = _lru_prompt(p, B, S, cw, cb, wr, br, wi, bi, lam)
        o_c = _mla_prompt(mq, mk, p, B, S, wuv)
        o_d = _dsa_prompt(dqb, dqib, kib, kvb, p, B, S, topk_p)
        xp = _merge(xp, o_a, o_b, o_c, o_d, p, wb, wo, fg, final)
        outs[0].append(rows.reshape(B, S, MLA_ROW))
        outs[2].append(kvf.reshape(B, S, 2, DSA_KV_HEADS, DSA_HD))
        outs[4].append(kif.reshape(B, S, DSA_IDX_DIM))
        outs[6].append(gla_s)
        outs[8].append(lru_h.reshape(B, LRU_W))
        outs[10].append(conv_new)

        p = _inproj(xs, gain, w_l)
        rows, mk, mq, kvf, kvb, kif, kib, dqb, dqib = _prep(p, cos_s, sin_s, qn, wuq, kvn, wuk)
        p2 = p.reshape(DB, SUB * W_PAD)
        p3 = p.reshape(DB, SUB, W_PAD)
        o_a, gla_s = _gla_sample(p2, state_gla[l], T, wg2, bg, gn)
        o_b, lru_h, conv_new = _lru_sample(p2, state_lru_conv[l].reshape(DB, -1), state_lru_h[l], T,
                                           cw, cb, wr, br, wi, bi, lam)
        o_c = _mla_sample(page_table, cache_mla_t, l, mq.reshape(DB, SUB, -1), mk.reshape(DB, SUB, -1), p3, wuv)
        o_d = _dsa_sample(page_table, cache_ki_t, cache_kv, l, dqb.reshape(DB, SUB, -1),
                          dqib.reshape(DB, SUB, -1), p3, topk_s)
        xs = _merge(xs, o_a.reshape(DB * SUB, -1), o_b.reshape(DB * SUB, -1), o_c.reshape(DB * SUB, -1),
                    o_d.reshape(DB * SUB, -1), p, wb, wo, fg, final)
        outs[1].append(rows.reshape(DB, SUB, MLA_ROW)[:, :T])
        outs[3].append(kvf.reshape(DB, SUB, 2, DSA_KV_HEADS, DSA_HD)[:, :T])
        outs[5].append(kif.reshape(DB, SUB, DSA_IDX_DIM)[:, :T])
        outs[7].append(gla_s)
        outs[9].append(lru_h)
        outs[11].append(conv_new.reshape(DB, CONV_W - 1, LRU_W))

    y_prompt = xp.reshape(B, S, D)
    y_sample = xs.reshape(DB, SUB, D)[:, :T]
    return (y_prompt, y_sample) + tuple(jnp.stack(o) for o in outs)
```

```python
import functools
import math

import numpy as np
import jax
import jax.numpy as jnp
from jax import lax
from jax.experimental import pallas as pl
from jax.experimental.pallas import tpu as pltpu

f32 = jnp.float32
bf16 = jnp.bfloat16
MXU_DTYPE = bf16

LANE = 128
SUB = 8
VMEM_LIMIT = 56 * 1024 * 1024
PAGES_PER_STEP = 16

D_MODEL = 1024
PAGE_SIZE = 128
N_BRANCH = 4
BRANCH_W = D_MODEL // 2
GLA_HEADS = 4
GLA_DV = BRANCH_W // GLA_HEADS
GLA_DK = GLA_DV // 2
GLA_GATE_RANK = 16
GLA_TAU = 16.0
GLA_CHUNK = 64
LRU_W = BRANCH_W
LRU_BLOCKS = 8
LRU_BW = LRU_W // LRU_BLOCKS
LRU_C = 8.0
CONV_W = 4
MLA_HEADS = 4
MLA_NOPE = 64
MLA_ROPE = 32
MLA_V = BRANCH_W // MLA_HEADS
MLA_Q_LORA = D_MODEL // 4
MLA_KV_LORA = D_MODEL // 4
MLA_ROW = MLA_KV_LORA + MLA_ROPE
MLA_SCALE = (MLA_NOPE + MLA_ROPE) ** -0.5
MLA_KPAD = MLA_KV_LORA + LANE
ROPE_THETA = 10000.0
DSA_HEADS = 4
DSA_KV_HEADS = 1
DSA_HD = BRANCH_W // DSA_HEADS
DSA_IDX_HEADS = 4
DSA_IDX_DIM = 64
DSA_TOPK = 256
EPS = 1e-6
IN_SIZES = (
    GLA_HEADS * GLA_DK, GLA_HEADS * GLA_DK, GLA_HEADS * GLA_DV, GLA_GATE_RANK, BRANCH_W,
    LRU_W, LRU_W,
    MLA_Q_LORA, MLA_KV_LORA, MLA_ROPE, BRANCH_W,
    DSA_HEADS * DSA_HD, DSA_KV_HEADS * DSA_HD, DSA_KV_HEADS * DSA_HD,
    DSA_IDX_HEADS * DSA_IDX_DIM, DSA_IDX_DIM, DSA_IDX_HEADS, BRANCH_W,
    N_BRANCH * D_MODEL,
)
(_GQ, _GK, _GV, _GLR, _GGATE, _LX, _LGATE, _CQ, _CKV, _KR, _CGATE,
 _DQ, _DK, _DV, _DQI, _DKI, _DWI, _DGATE, _MERGE) = range(len(IN_SIZES))
_IN_OFFS = np.concatenate([[0], np.cumsum(IN_SIZES)]).tolist()

OFF_MERGE = 0
OFF_GV = 4096
OFF_GGATE = 4608
OFF_LX = 5120
OFF_LGATE = 5632
OFF_CGATE = 6144
OFF_DQ = 6656
OFF_DGATE = 7168
OFF_DQI = 7680
OFF_GQ = 8192
OFF_GK = 8448
OFF_CQ = 8704
OFF_CKV = 8960
OFF_DKV = 9216
OFF_GLR = 9472
OFF_KR = 9600
OFF_DKI = 9728
W_PAD = 10240

KEY_NEG_INF = -2139095041
INT_MIN = -2147483648

NT_DIMS = (((1,), (1,)), ((), ()))
TN_DIMS = (((0,), (0,)), ((), ()))


def _cparams(*sem):
    return pltpu.CompilerParams(dimension_semantics=sem, vmem_limit_bytes=VMEM_LIMIT)


def _mm(a, b):
    return jnp.dot(a.astype(MXU_DTYPE), b.astype(MXU_DTYPE), preferred_element_type=f32)


def _mm_nt(a, b):
    return lax.dot_general(a.astype(MXU_DTYPE), b.astype(MXU_DTYPE), NT_DIMS, preferred_element_type=f32)


def _mm_tn(a, b):
    return lax.dot_general(a.astype(MXU_DTYPE), b.astype(MXU_DTYPE), TN_DIMS, preferred_element_type=f32)


def _rms(x, g):
    return x * lax.rsqrt(jnp.mean(x * x, axis=-1, keepdims=True) + EPS) * g


def _silu(x):
    return x * jax.nn.sigmoid(x)


def _softplus(x):
    return jnp.maximum(x, 0.0) + jnp.log1p(jnp.exp(-jnp.abs(x)))


def _neg_expm1(y):
    acc = jnp.full_like(y, 1.0 / 479001600.0)
    for k in (39916800.0, 3628800.0, 362880.0, 40320.0, 5040.0, 720.0, 120.0, 24.0, 6.0, 2.0, 1.0):
        acc = acc * y + 1.0 / k
    return jnp.where(y > -0.25, -(acc * y), 1.0 - jnp.exp(y))


def _sort_key(x):
    b = lax.bitcast_convert_type(x, jnp.int32)
    return b ^ ((b >> 31) & jnp.int32(0x7FFFFFFF))


def _softmax_update(s, v, m_ref, l_ref, acc_ref, v_transposed=False):
    m_old = m_ref[...]
    m_new = jnp.maximum(m_old, jnp.max(s, axis=1, keepdims=True))
    m_safe = jnp.where(m_new == -jnp.inf, 0.0, m_new)
    alpha = jnp.exp(m_old - m_safe)
    p = jnp.exp(s - m_safe)
    l_ref[...] = alpha * l_ref[...] + jnp.sum(p, axis=1, keepdims=True)
    acc_ref[...] = alpha * acc_ref[...] + (_mm_nt(p, v) if v_transposed else _mm(p, v))
    m_ref[...] = m_new


def _softmax_init(m_ref, l_ref, acc_ref):
    m_ref[...] = jnp.full(m_ref.shape, -jnp.inf, f32)
    l_ref[...] = jnp.zeros(l_ref.shape, f32)
    acc_ref[...] = jnp.zeros(acc_ref.shape, f32)


def _stack_heads(x, n, w):
    return jnp.concatenate([x[:, h * w:(h + 1) * w] for h in range(n)], axis=0)


def _inproj_kernel(x_ref, g_ref, w_ref, o_ref, h_ref):
    @pl.when(pl.program_id(1) == 0)
    def _():
        h_ref[...] = _rms(x_ref[...], g_ref[...]).astype(h_ref.dtype)

    o_ref[...] = lax.dot_general(h_ref[...], w_ref[...], NT_DIMS, preferred_element_type=f32)


def _inproj(x, gain, w_t):
    n, d = x.shape
    tm = min(n, 1024)
    tn = 512
    return pl.pallas_call(
        _inproj_kernel,
        grid=(n // tm, W_PAD // tn),
        in_specs=[pl.BlockSpec((tm, d), lambda i, j: (i, 0)),
                  pl.BlockSpec((1, d), lambda i, j: (0, 0)),
                  pl.BlockSpec((tn, d), lambda i, j: (j, 0))],
        out_specs=pl.BlockSpec((tm, tn), lambda i, j: (i, j)),
        out_shape=jax.ShapeDtypeStruct((n, W_PAD), f32),
        scratch_shapes=[pltpu.VMEM((tm, d), MXU_DTYPE)],
        compiler_params=_cparams("parallel", "arbitrary"),
        name="inproj",
    )(x, gain, w_t)


def _prep_kernel(cq_ref, ckv_ref, kr_ref, dq_ref, dqi_ref, dkv_ref, dki_ref, cos_ref, sin_ref,
                 qn_ref, wuq_ref, kvn_ref, wuk_ref,
                 rows_ref, mk_ref, mq_ref, kvf_ref, kvb_ref, kif_ref, kib_ref, dqb_ref, dqib_ref):
    tm = cq_ref.shape[0]
    cos = cos_ref[...]
    sin = sin_ref[...]
    lane = lax.broadcasted_iota(jnp.int32, (tm, LANE), 1)
    first_half = (lane % MLA_ROPE) < (MLA_ROPE // 2)

    def rope(x):
        swapped = jnp.where(first_half, pltpu.roll(x, LANE - MLA_ROPE // 2, 1), pltpu.roll(x, MLA_ROPE // 2, 1))
        return x * cos + swapped * sin

    c = _rms(ckv_ref[...], kvn_ref[...])
    kr = rope(kr_ref[...])
    rows_ref[:, 0:MLA_KV_LORA] = c
    rows_ref[:, MLA_KV_LORA:MLA_ROW] = kr[:, 0:MLA_ROPE]
    mk_ref[:, 0:MLA_KV_LORA] = c.astype(mk_ref.dtype)
    mk_ref[:, MLA_KV_LORA:MLA_KPAD] = kr.astype(mk_ref.dtype)

    q = _mm(_rms(cq_ref[...], qn_ref[...]), wuq_ref[...])
    q_nope = q[:, 0:MLA_HEADS * MLA_NOPE]
    q_rope = rope(q[:, MLA_HEADS * MLA_NOPE:])
    lane_q = lax.broadcasted_iota(jnp.int32, q_nope.shape, 1)
    wuk = wuk_ref[...]
    for h in range(MLA_HEADS):
        q_h = jnp.where(lane_q // MLA_NOPE == h, q_nope, 0.0)
        q_lat = _mm(q_h, wuk)
        base = h * MLA_KPAD
        mq_ref[:, base:base + MLA_KV_LORA] = (q_lat * MLA_SCALE).astype(mq_ref.dtype)
        qr = q_rope if h == 0 else pltpu.roll(q_rope, LANE - h * MLA_ROPE, 1)
        qr = jnp.where(lane < MLA_ROPE, qr, 0.0)
        mq_ref[:, base + MLA_KV_LORA:base + MLA_KPAD] = (qr * MLA_SCALE).astype(mq_ref.dtype)

    dkv = dkv_ref[...]
    kvf_ref[...] = dkv
    kvb_ref[...] = dkv.astype(kvb_ref.dtype)
    seg = dki_ref[...]
    kif_ref[...] = seg[:, 0:DSA_IDX_DIM]
    kib_ref[...] = jnp.where(lane < DSA_IDX_DIM, seg, 0.0).astype(kib_ref.dtype)
    dqb_ref[...] = (dq_ref[...] * (DSA_HD ** -0.5)).astype(dqb_ref.dtype)
    dqib_ref[...] = (dqi_ref[...] * (DSA_IDX_DIM ** -0.5)).astype(dqib_ref.dtype)


def _prep(p, cos_t, sin_t, q_norm, w_uq, kv_norm, w_uk):
    n = p.shape[0]
    tm = min(n, 256)
    nt = cos_t.shape[0] // tm

    def seg(off, w):
        return pl.BlockSpec((tm, w), lambda i: (i, off // w))

    def const(a):
        return pl.BlockSpec(a.shape, lambda i: (0,) * a.ndim)

    def out(w):
        return pl.BlockSpec((tm, w), lambda i: (i, 0))

    outs = [(MLA_ROW, f32), (MLA_KPAD, MXU_DTYPE), (MLA_HEADS * MLA_KPAD, MXU_DTYPE),
            (2 * DSA_HD, f32), (2 * DSA_HD, MXU_DTYPE), (DSA_IDX_DIM, f32), (LANE, MXU_DTYPE),
            (DSA_HEADS * DSA_HD, MXU_DTYPE), (DSA_IDX_HEADS * LANE, MXU_DTYPE)]
    return pl.pallas_call(
        _prep_kernel,
        grid=(n // tm,),
        in_specs=[seg(OFF_CQ, 256), seg(OFF_CKV, 256), seg(OFF_KR, LANE), seg(OFF_DQ, 512),
                  seg(OFF_DQI, 512), seg(OFF_DKV, 256), seg(OFF_DKI, LANE),
                  pl.BlockSpec((tm, LANE), lambda i: (i % nt, 0)),
                  pl.BlockSpec((tm, LANE), lambda i: (i % nt, 0)),
                  const(q_norm), const(w_uq), const(kv_norm), const(w_uk)],
        out_specs=[out(w) for w, _ in outs],
        out_shape=[jax.ShapeDtypeStruct((n, w), dt) for w, dt in outs],
        compiler_params=_cparams("parallel"),
        name="prep",
    )(p, p, p, p, p, p, p, cos_t, sin_t, q_norm, w_uq, kv_norm, w_uk)


def _gla_gate(glr, wg2, bg):
    z = _mm(glr, wg2) + bg
    return (jnp.minimum(z, 0.0) - jnp.log1p(jnp.exp(-jnp.abs(z)))) * (1.0 / GLA_TAU)


def _gla_out(o, gate, gn):
    return _rms(o, gn) * _silu(gate)


def _gla_prompt_kernel(q_ref, k_ref, v_ref, gate_ref, glr_ref, wg2_ref, bg_ref, gn_ref,
                       o_ref, s_ref, st_ref, *, n_chunks):
    t = pl.program_id(1)

    @pl.when(t == 0)
    def _():
        st_ref[...] = jnp.zeros(st_ref.shape, f32)

    C = GLA_CHUNK
    HK = GLA_HEADS * GLA_DK
    r_i = lax.broadcasted_iota(jnp.int32, (C, C), 0)
    c_i = lax.broadcasted_iota(jnp.int32, (C, C), 1)
    causal = r_i >= c_i
    tril = causal.astype(f32)
    lane_head = lax.broadcasted_iota(jnp.int32, (C, HK), 1) // GLA_DK
    lane_head_st = lax.broadcasted_iota(jnp.int32, (GLA_DV, HK), 1) // GLA_DK
    causal_h = jnp.concatenate([causal] * GLA_HEADS, axis=0)
    gn = gn_ref[...]
    wg2 = wg2_ref[...]
    bg = bg_ref[...]
    for c in range(n_chunks):
        rows = slice(c * C, (c + 1) * C)
        g = _gla_gate(glr_ref[rows, :], wg2, bg)
        b = jnp.dot(tril, g, precision=lax.Precision.HIGHEST, preferred_element_type=f32)
        b_last = b[C - 1:C, :]
        q = q_ref[rows, :] * (GLA_DK ** -0.5)
        k = k_ref[rows, :]
        v = v_ref[rows, :]
        q_dec = q * jnp.exp(b)
        k_dec = k * jnp.exp(-b)
        k_tail = k * jnp.exp(b_last - b)
        decay = jnp.exp(b_last)
        q_heads = jnp.concatenate(
            [jnp.where(lane_head == h, q_dec, 0.0) for h in range(GLA_HEADS)], axis=0)
        att = jnp.where(causal_h, _mm_nt(q_heads, k_dec), 0.0)
        st = st_ref[...]
        o_inter = _mm_nt(q_heads, st)
        gate = gate_ref[rows, :]
        for h in range(GLA_HEADS):
            hv = slice(h * GLA_DV, (h + 1) * GLA_DV)
            o_h = _mm(att[h * C:(h + 1) * C, :], v[:, hv]) + o_inter[h * C:(h + 1) * C, :]
            o_ref[rows, hv] = _gla_out(o_h, gate[:, hv], gn)
        kv = _mm_tn(v, k_tail)
        upd = jnp.zeros((GLA_DV, HK), f32)
        for h in range(GLA_HEADS):
            upd = upd + jnp.where(lane_head_st == h, kv[h * GLA_DV:(h + 1) * GLA_DV, :], 0.0)
        st_ref[...] = decay * st + upd

    @pl.when(t == pl.num_programs(1) - 1)
    def _():
        s_ref[...] = st_ref[...].T.reshape(GLA_HEADS, GLA_DK, GLA_DV)


def _gla_prompt(p, B, L, wg2, bg, gn):
    tl = min(L, 256)
    nt = L // tl

    def seg(off, w):
        return pl.BlockSpec((tl, w), lambda b, t: (b * nt + t, off // w))

    def const(a):
        return pl.BlockSpec(a.shape, lambda b, t: (0,) * a.ndim)

    return pl.pallas_call(
        functools.partial(_gla_prompt_kernel, n_chunks=tl // GLA_CHUNK),
        grid=(B, nt),
        in_specs=[seg(OFF_GQ, 256), seg(OFF_GK, 256), seg(OFF_GV, 512), seg(OFF_GGATE, 512),
                  seg(OFF_GLR, LANE), const(wg2), const(bg), const(gn)],
        out_specs=[pl.BlockSpec((tl, BRANCH_W), lambda b, t: (b * nt + t, 0)),
                   pl.BlockSpec((None, GLA_HEADS, GLA_DK, GLA_DV), lambda b, t: (b, 0, 0, 0))],
        out_shape=[jax.ShapeDtypeStruct((B * L, BRANCH_W), f32),
                   jax.ShapeDtypeStruct((B, GLA_HEADS, GLA_DK, GLA_DV), f32)],
        scratch_shapes=[pltpu.VMEM((GLA_DV, GLA_HEADS * GLA_DK), f32)],
        compiler_params=_cparams("parallel", "arbitrary"),
        name="gla_prompt",
    )(p, p, p, p, p, wg2, bg, gn)


def _gla_sample_kernel(*refs, n_tok, bg_rows):
    q_refs = refs[0:n_tok]
    k_refs = refs[n_tok:2 * n_tok]
    v_refs = refs[2 * n_tok:3 * n_tok]
    gate_refs = refs[3 * n_tok:4 * n_tok]
    glr_refs = refs[4 * n_tok:5 * n_tok]
    s0_ref, wg2_ref, bg_ref, gn_ref, o_ref, s_ref, tr_ref, oraw_ref = refs[5 * n_tok:]
    HK = GLA_HEADS * GLA_DK
    zpad = jnp.zeros((LANE - bg_rows, HK), f32)

    def col_major(x):
        return jnp.concatenate([x, zpad], axis=0).T

    for t in range(n_tok):
        g = _gla_gate(glr_refs[t][...], wg2_ref[...], bg_ref[...])
        tr_ref[3 * t + 0] = col_major(q_refs[t][...] * (GLA_DK ** -0.5))
        tr_ref[3 * t + 1] = col_major(k_refs[t][...])
        tr_ref[3 * t + 2] = col_major(jnp.exp(g))
    for i in range(bg_rows):
        s = s0_ref[i].reshape(HK, GLA_DV)
        for t in range(n_tok):
            qc = tr_ref[3 * t + 0, :, i:i + 1]
            kc = tr_ref[3 * t + 1, :, i:i + 1]
            ec = tr_ref[3 * t + 2, :, i:i + 1]
            vrow = v_refs[t][i:i + 1, :]
            vb = jnp.concatenate(
                [jnp.broadcast_to(vrow[:, h * GLA_DV:(h + 1) * GLA_DV], (GLA_DK, GLA_DV))
                 for h in range(GLA_HEADS)], axis=0)
            s = ec * s + kc * vb
            prod = qc * s
            oraw_ref[t, i:i + 1, :] = jnp.concatenate(
                [jnp.sum(prod[h * GLA_DK:(h + 1) * GLA_DK, :], axis=0, keepdims=True)
                 for h in range(GLA_HEADS)], axis=1)
        s_ref[i] = s.reshape(GLA_HEADS, GLA_DK, GLA_DV)
    gn = gn_ref[...]
    for t in range(n_tok):
        o = oraw_ref[t]
        gate = gate_refs[t][...]
        for h in range(GLA_HEADS):
            hv = slice(h * GLA_DV, (h + 1) * GLA_DV)
            o_ref[:, t * BRANCH_W + h * GLA_DV:t * BRANCH_W + (h + 1) * GLA_DV] = _gla_out(o[:, hv], gate[:, hv], gn)
    o_ref[:, n_tok * BRANCH_W:] = jnp.zeros((bg_rows, (SUB - n_tok) * BRANCH_W), f32)


def _gla_sample(p2, s0, n_tok, wg2, bg, gn):
    db = p2.shape[0]
    bgr = min(db, 16)

    def seg(t, off, w):
        return pl.BlockSpec((bgr, w), lambda i: (i, (t * W_PAD + off) // w))

    def const(a):
        return pl.BlockSpec(a.shape, lambda i: (0,) * a.ndim)

    in_specs = ([seg(t, OFF_GQ, 256) for t in range(n_tok)] + [seg(t, OFF_GK, 256) for t in range(n_tok)]
                + [seg(t, OFF_GV, 512) for t in range(n_tok)] + [seg(t, OFF_GGATE, 512) for t in range(n_tok)]
                + [seg(t, OFF_GLR, LANE) for t in range(n_tok)]
                + [pl.BlockSpec((bgr, GLA_HEADS, GLA_DK, GLA_DV), lambda i: (i, 0, 0, 0)),
                   const(wg2), const(bg), const(gn)])
    return pl.pallas_call(
        functools.partial(_gla_sample_kernel, n_tok=n_tok, bg_rows=bgr),
        grid=(db // bgr,),
        in_specs=in_specs,
        out_specs=[pl.BlockSpec((bgr, SUB * BRANCH_W), lambda i: (i, 0)),
                   pl.BlockSpec((bgr, GLA_HEADS, GLA_DK, GLA_DV), lambda i: (i, 0, 0, 0))],
        out_shape=[jax.ShapeDtypeStruct((db, SUB * BRANCH_W), f32),
                   jax.ShapeDtypeStruct(s0.shape, f32)],
        scratch_shapes=[pltpu.VMEM((3 * n_tok, GLA_HEADS * GLA_DK, LANE), f32),
                        pltpu.VMEM((n_tok, bgr, BRANCH_W), f32)],
        compiler_params=_cparams("parallel"),
        name="gla_sample",
    )(*([p2] * (5 * n_tok)), s0, wg2, bg, gn)


def _lru_gates(xc, wr, br, wi, bi, sp):
    r = jax.nn.sigmoid(_mm(xc, wr) + br)
    ig = jax.nn.sigmoid(_mm(xc, wi) + bi)
    log_a = -LRU_C * r * sp
    return jnp.exp(log_a), jnp.sqrt(_neg_expm1(2.0 * log_a)) * (ig * xc)


def _lru_prompt_kernel(x_ref, gate_ref, cw_ref, cb_ref, wr_ref, br_ref, wi_ref, bi_ref, lam_ref,
                       o_ref, hfin_ref, conv_ref, tail_ref, h_ref, a_ref, u_ref, hs_ref):
    t = pl.program_id(1)
    tl = x_ref.shape[0]

    @pl.when(t == 0)
    def _():
        tail_ref[...] = jnp.zeros(tail_ref.shape, f32)
        h_ref[...] = jnp.zeros(h_ref.shape, f32)

    x = x_ref[...]
    ext = jnp.concatenate([tail_ref[...], x], axis=0)
    cw = cw_ref[...]
    xc = jnp.zeros(x.shape, f32)
    for j in range(CONV_W):
        s = SUB - (CONV_W - 1) + j
        xc = xc + ext[s:s + tl, :] * cw[j:j + 1, :]
    xc = cb_ref[...] + xc
    a, u = _lru_gates(xc, wr_ref[...], br_ref[...], wi_ref[...], bi_ref[...], _softplus(-lam_ref[...]))
    a_ref[...] = a
    u_ref[...] = u

    def step(i, h):
        h = a_ref[pl.ds(i, 1), :] * h + u_ref[pl.ds(i, 1), :]
        hs_ref[pl.ds(i, 1), :] = h
        return h

    h = lax.fori_loop(0, tl, step, h_ref[...], unroll=8)
    h_ref[...] = h
    tail_ref[...] = x[tl - SUB:, :]
    o_ref[...] = hs_ref[...] * _silu(gate_ref[...])

    @pl.when(t == pl.num_programs(1) - 1)
    def _():
        hfin_ref[...] = h
        conv_ref[...] = x[tl - (CONV_W - 1):, :]


def _lru_prompt(p, B, L, cw, cb, wr, br, wi, bi, lam):
    tl = min(L, 256)
    nt = L // tl

    def seg(off, w):
        return pl.BlockSpec((tl, w), lambda b, t: (b * nt + t, off // w))

    def const(a):
        return pl.BlockSpec(a.shape, lambda b, t: (0,) * a.ndim)

    return pl.pallas_call(
        _lru_prompt_kernel,
        grid=(B, nt),
        in_specs=[seg(OFF_LX, 512), seg(OFF_LGATE, 512), const(cw), const(cb), const(wr), const(br),
                  const(wi), const(bi), const(lam)],
        out_specs=[pl.BlockSpec((tl, LRU_W), lambda b, t: (b * nt + t, 0)),
                   pl.BlockSpec((None, 1, LRU_W), lambda b, t: (b, 0, 0)),
                   pl.BlockSpec((None, CONV_W - 1, LRU_W), lambda b, t: (b, 0, 0))],
        out_shape=[jax.ShapeDtypeStruct((B * L, LRU_W), f32),
                   jax.ShapeDtypeStruct((B, 1, LRU_W), f32),
                   jax.ShapeDtypeStruct((B, CONV_W - 1, LRU_W), f32)],
        scratch_shapes=[pltpu.VMEM((SUB, LRU_W), f32), pltpu.VMEM((1, LRU_W), f32),
                        pltpu.VMEM((tl, LRU_W), f32), pltpu.VMEM((tl, LRU_W), f32),
                        pltpu.VMEM((tl, LRU_W), f32)],
        compiler_params=_cparams("parallel", "arbitrary"),
        name="lru_prompt",
    )(p, p, cw, cb, wr, br, wi, bi, lam)


def _lru_sample_kernel(*refs, n_tok):
    x_refs = refs[0:n_tok]
    gate_refs = refs[n_tok:2 * n_tok]
    (buf_ref, h0_ref, cw_ref, cb_ref, wr_ref, br_ref, wi_ref, bi_ref, lam_ref,
     o_ref, hfin_ref, conv_ref) = refs[2 * n_tok:]
    W = LRU_W
    xs = [buf_ref[:, j * W:(j + 1) * W] for j in range(CONV_W - 1)] + [r[...] for r in x_refs]
    cw = cw_ref[...]
    sp = _softplus(-lam_ref[...])
    h = h0_ref[...]
    for t in range(n_tok):
        xc = jnp.zeros(h.shape, f32)
        for j in range(CONV_W):
            xc = xc + xs[t + j] * cw[j:j + 1, :]
        xc = cb_ref[...] + xc
        a, u = _lru_gates(xc, wr_ref[...], br_ref[...], wi_ref[...], bi_ref[...], sp)
        h = a * h + u
        o_ref[:, t * W:(t + 1) * W] = h * _silu(gate_refs[t][...])
    o_ref[:, n_tok * W:] = jnp.zeros((h.shape[0], (SUB - n_tok) * W), f32)
    hfin_ref[...] = h
    for j in range(CONV_W - 1):
        conv_ref[:, j * W:(j + 1) * W] = xs[n_tok + j]


def _lru_sample(p2, buf, h0, n_tok, cw, cb, wr, br, wi, bi, lam):
    db = p2.shape[0]

    def seg(t, off, w):
        return pl.BlockSpec((db, w), lambda i: (0, (t * W_PAD + off) // w))

    def const(a):
        return pl.BlockSpec(a.shape, lambda i: (0,) * a.ndim)

    in_specs = ([seg(t, OFF_LX, 512) for t in range(n_tok)] + [seg(t, OFF_LGATE, 512) for t in range(n_tok)]
                + [const(a) for a in (buf, h0, cw, cb, wr, br, wi, bi, lam)])
    return pl.pallas_call(
        functools.partial(_lru_sample_kernel, n_tok=n_tok),
        grid=(1,),
        in_specs=in_specs,
        out_specs=[pl.BlockSpec((db, SUB * LRU_W), lambda i: (0, 0)),
                   pl.BlockSpec((db, LRU_W), lambda i: (0, 0)),
                   pl.BlockSpec((db, (CONV_W - 1) * LRU_W), lambda i: (0, 0))],
        out_shape=[jax.ShapeDtypeStruct((db, SUB * LRU_W), f32),
                   jax.ShapeDtypeStruct((db, LRU_W), f32),
                   jax.ShapeDtypeStruct((db, (CONV_W - 1) * LRU_W), f32)],
        compiler_params=_cparams("arbitrary"),
        name="lru_sample",
    )(*([p2] * (2 * n_tok)), buf, h0, cw, cb, wr, br, wi, bi, lam)


def _mla_finish(acc_ref, l_ref, wuv_ref, gate, o_ref, rows):
    o = acc_ref[...] / l_ref[...]
    for h in range(MLA_HEADS):
        oc = _mm(o[h * rows:(h + 1) * rows, :], wuv_ref[h])
        o_ref[:, h * MLA_V:(h + 1) * MLA_V] = oc * _silu(gate[:, h * MLA_V:(h + 1) * MLA_V])


def _mla_prompt_kernel(q_ref, k_ref, wuv_ref, gate_ref, o_ref, m_ref, l_ref, acc_ref, *, tq, tk):
    i = pl.program_id(1)
    qs = _stack_heads(q_ref[...], MLA_HEADS, MLA_KPAD)
    _softmax_init(m_ref, l_ref, acc_ref)
    n_full = (i * tq) // tk

    def keys(j):
        return k_ref[pl.ds(pl.multiple_of(j * tk, tk), tk), :]

    def body(j, s):
        s_next = _mm_nt(qs, keys(j + 1))
        _softmax_update(s, keys(j)[:, 0:MLA_KV_LORA], m_ref, l_ref, acc_ref)
        return s_next

    s = lax.fori_loop(0, n_full, body, _mm_nt(qs, keys(0)))
    row = i * tq + lax.broadcasted_iota(jnp.int32, s.shape, 0) % tq
    col = n_full * tk + lax.broadcasted_iota(jnp.int32, s.shape, 1)
    _softmax_update(jnp.where(col <= row, s, -jnp.inf), keys(n_full)[:, 0:MLA_KV_LORA], m_ref, l_ref, acc_ref)
    _mla_finish(acc_ref, l_ref, wuv_ref, gate_ref[...], o_ref, tq)


def _mla_prompt(mq, mk, p, B, L, wuv):
    tq = 128
    tk = min(L, 512)
    nq = L // tq
    rows = MLA_HEADS * tq
    return pl.pallas_call(
        functools.partial(_mla_prompt_kernel, tq=tq, tk=tk),
        grid=(B, nq),
        in_specs=[pl.BlockSpec((tq, MLA_HEADS * MLA_KPAD), lambda b, i: (b * nq + i, 0)),
                  pl.BlockSpec((L, MLA_KPAD), lambda b, i: (b, 0)),
                  pl.BlockSpec(wuv.shape, lambda b, i: (0, 0, 0)),
                  pl.BlockSpec((tq, BRANCH_W), lambda b, i: (b * nq + i, OFF_CGATE // BRANCH_W))],
        out_specs=pl.BlockSpec((tq, BRANCH_W), lambda b, i: (b * nq + i, 0)),
        out_shape=jax.ShapeDtypeStruct((B * L, BRANCH_W), f32),
        scratch_shapes=[pltpu.VMEM((rows, 1), f32), pltpu.VMEM((rows, 1), f32),
                        pltpu.VMEM((rows, MLA_KV_LORA), f32)],
        compiler_params=_cparams("parallel", "arbitrary"),
        name="mla_prompt",
    )(mq, mk, wuv, p)


def _new_block(x):
    return jnp.concatenate([x.astype(f32), jnp.zeros((LANE - SUB, x.shape[1]), f32)], axis=0)


def _mla_sample_kernel(pt_ref, q_ref, knew_ref, wuv_ref, gate_ref, *rest, n_grp):
    page_refs = rest[0:n_grp]
    o_ref, m_ref, l_ref, acc_ref = rest[n_grp:]
    j = pl.program_id(1)

    @pl.when(j == 0)
    def _():
        _softmax_init(m_ref, l_ref, acc_ref)

    qs = _stack_heads(q_ref[...], MLA_HEADS, MLA_KPAD)
    k_t = jnp.concatenate([r[...].astype(MXU_DTYPE) for r in page_refs], axis=1)
    k_t_pad = jnp.concatenate([k_t, jnp.zeros((MLA_KPAD - MLA_ROW, k_t.shape[1]), MXU_DTYPE)], axis=0)
    _softmax_update(_mm(qs, k_t_pad), k_t[0:MLA_KV_LORA, :], m_ref, l_ref, acc_ref, v_transposed=True)

    @pl.when(j == pl.num_programs(1) - 1)
    def _():
        kn = _new_block(knew_ref[...]).astype(MXU_DTYPE)
        s = _mm_nt(qs, kn)
        tok = lax.broadcasted_iota(jnp.int32, s.shape, 0) % SUB
        col = lax.broadcasted_iota(jnp.int32, s.shape, 1)
        _softmax_update(jnp.where(col <= tok, s, -jnp.inf), kn[:, 0:MLA_KV_LORA], m_ref, l_ref, acc_ref)
        _mla_finish(acc_ref, l_ref, wuv_ref, gate_ref[...], o_ref, SUB)


def _mla_sample(page_table, cache, layer, mq3, mk3, p3, wuv):
    db, n_pages = page_table.shape
    n_grp = math.gcd(n_pages, PAGES_PER_STEP)
    n_steps = n_pages // n_grp
    rows = MLA_HEADS * SUB

    def page_spec(g):
        return pl.BlockSpec((None, None, MLA_ROW, PAGE_SIZE),
                            lambda b, j, pt: (layer, pt[b * n_pages + j * n_grp + g], 0, 0))

    grid_spec = pltpu.PrefetchScalarGridSpec(
        num_scalar_prefetch=1,
        grid=(db, n_steps),
        in_specs=[pl.BlockSpec((None, SUB, MLA_HEADS * MLA_KPAD), lambda b, j, pt: (b, 0, 0)),
                  pl.BlockSpec((None, SUB, MLA_KPAD), lambda b, j, pt: (b, 0, 0)),
                  pl.BlockSpec(wuv.shape, lambda b, j, pt: (0, 0, 0)),
                  pl.BlockSpec((None, SUB, BRANCH_W), lambda b, j, pt: (b, 0, OFF_CGATE // BRANCH_W))]
                 + [page_spec(g) for g in range(n_grp)],
        out_specs=pl.BlockSpec((None, SUB, BRANCH_W), lambda b, j, pt: (b, 0, 0)),
        scratch_shapes=[pltpu.VMEM((rows, 1), f32), pltpu.VMEM((rows, 1), f32),
                        pltpu.VMEM((rows, MLA_KV_LORA), f32)])
    return pl.pallas_call(
        functools.partial(_mla_sample_kernel, n_grp=n_grp),
        grid_spec=grid_spec,
        out_shape=jax.ShapeDtypeStruct((db, SUB, BRANCH_W), f32),
        compiler_params=_cparams("parallel", "arbitrary"),
        name="mla_sample",
    )(page_table.reshape(-1), mq3, mk3, wuv, p3, *([cache] * n_grp))


def _index_scores(qi_stack, wcols, keys_b, rows, keys_transposed=False):
    return _weighted_relu(_mm(qi_stack, keys_b) if keys_transposed else _mm_nt(qi_stack, keys_b), wcols, rows)


def _weighted_relu(raw, wcols, rows):
    sc = jnp.maximum(raw, 0.0)
    tot = sc[0:rows, :] * wcols[0]
    for h in range(1, DSA_IDX_HEADS):
        tot = tot + sc[h * rows:(h + 1) * rows, :] * wcols[h]
    return jnp.where(tot == 0.0, 0.0, tot)


def _wi_cols(seg):
    scale = DSA_IDX_HEADS ** -0.5
    return [seg[:, DSA_IDX_DIM + h:DSA_IDX_DIM + h + 1] * scale for h in range(DSA_IDX_HEADS)]


def _kth_largest(count_ge, rows, topk):
    t0 = jnp.where(count_ge(jnp.zeros((rows, 1), jnp.int32)) >= topk,
                   jnp.int32(0), jnp.int32(INT_MIN))

    def bit_body(it, t):
        cand = t + lax.shift_left(jnp.int32(1), jnp.int32(30) - it)
        return jnp.where(count_ge(cand) >= topk, cand, t)

    return lax.fori_loop(0, 31, bit_body, t0)


def _dsa_finish(acc_ref, l_ref, gate, o_ref, rows):
    o = acc_ref[...] / l_ref[...]
    for h in range(DSA_HEADS):
        hv = slice(h * DSA_HD, (h + 1) * DSA_HD)
        o_ref[:, hv] = o[h * rows:(h + 1) * rows, :] * _silu(gate[:, hv])


def _dsa_prompt_kernel(q_ref, qi_ref, wi_ref, gate_ref, kib_ref, kvb_ref, u_ref, o_ref,
                       keys_ref, m_ref, l_ref, acc_ref, *, tq, tk, topk):
    i = pl.program_id(1)
    n_ch = (i * tq) // tk + 1
    qi = _stack_heads(qi_ref[...], DSA_IDX_HEADS, LANE)
    wcols = _wi_cols(wi_ref[...])
    row_pos = i * tq + lax.broadcasted_iota(jnp.int32, (tq, tk), 0)
    col_in = lax.broadcasted_iota(jnp.int32, (tq, tk), 1)

    def score_body(j, carry):
        kb = kib_ref[pl.ds(pl.multiple_of(j * tk, tk), tk), :]
        tot = _index_scores(qi, wcols, kb, tq)
        keys_ref[j] = _sort_key(jnp.where(j * tk + col_in <= row_pos, tot, -jnp.inf))
        return carry

    lax.fori_loop(0, n_ch, score_body, 0)

    def count(pred, bound):
        bound_b = jnp.broadcast_to(bound, (tq, LANE))

        def body(j, acc):
            for c in range(tk // LANE):
                acc = acc + jnp.where(pred(keys_ref[j, :, c * LANE:(c + 1) * LANE], bound_b), 1.0, 0.0)
            return acc

        acc = lax.fori_loop(0, n_ch, body, jnp.zeros((tq, LANE), f32))
        return jnp.sum(acc, axis=1, keepdims=True)

    thr = _kth_largest(lambda cand: count(lambda kk, bb: kk >= bb, cand), tq, topk)
    need = topk - count(lambda kk, bb: kk > bb, thr)
    tie_ok = thr > KEY_NEG_INF
    qs = _stack_heads(q_ref[...], DSA_HEADS, DSA_HD)
    _softmax_init(m_ref, l_ref, acc_ref)

    def kv_block(j):
        return kvb_ref[pl.ds(pl.multiple_of(j * tk, tk), tk), :]

    def qk(j):
        return _mm_nt(qs, kv_block(j)[:, 0:DSA_HD])

    def consume(j, s, run):
        kk = keys_ref[j]
        eq = (kk == thr) & tie_ok
        eqf = eq.astype(f32)
        before = _mm(eqf, u_ref[...]) + run
        sel = (kk > thr) | (eq & (before < need))
        s = jnp.where(jnp.concatenate([sel] * DSA_HEADS, axis=0), s, -jnp.inf)
        _softmax_update(s, kv_block(j)[:, DSA_HD:], m_ref, l_ref, acc_ref)
        return run + jnp.sum(eqf, axis=1, keepdims=True)

    def att_body(j, carry):
        s, run = carry
        s_next = qk(j + 1)
        return s_next, consume(j, s, run)

    s, run = lax.fori_loop(0, n_ch - 1, att_body, (qk(0), jnp.zeros((tq, 1), f32)))
    consume(n_ch - 1, s, run)
    _dsa_finish(acc_ref, l_ref, gate_ref[...], o_ref, tq)


def _dsa_prompt(dqb, dqib, kib, kvb, p, B, L, topk):
    tq = 128
    tk = min(L, 512)
    nq = L // tq
    rows = DSA_HEADS * tq
    upper = (np.arange(tk)[:, None] < np.arange(tk)[None, :]).astype(np.float32)
    u = jnp.asarray(upper, MXU_DTYPE)
    return pl.pallas_call(
        functools.partial(_dsa_prompt_kernel, tq=tq, tk=tk, topk=topk),
        grid=(B, nq),
        in_specs=[pl.BlockSpec((tq, DSA_HEADS * DSA_HD), lambda b, i: (b * nq + i, 0)),
                  pl.BlockSpec((tq, DSA_IDX_HEADS * LANE), lambda b, i: (b * nq + i, 0)),
                  pl.BlockSpec((tq, LANE), lambda b, i: (b * nq + i, OFF_DKI // LANE)),
                  pl.BlockSpec((tq, BRANCH_W), lambda b, i: (b * nq + i, OFF_DGATE // BRANCH_W)),
                  pl.BlockSpec((L, LANE), lambda b, i: (b, 0)),
                  pl.BlockSpec((L, 2 * DSA_HD), lambda b, i: (b, 0)),
                  pl.BlockSpec((tk, tk), lambda b, i: (0, 0))],
        out_specs=pl.BlockSpec((tq, BRANCH_W), lambda b, i: (b * nq + i, 0)),
        out_shape=jax.ShapeDtypeStruct((B * L, BRANCH_W), f32),
        scratch_shapes=[pltpu.VMEM((L // tk, tq, tk), jnp.int32),
                        pltpu.VMEM((rows, 1), f32), pltpu.VMEM((rows, 1), f32),
                        pltpu.VMEM((rows, DSA_HD), f32)],
        compiler_params=_cparams("parallel", "arbitrary"),
        name="dsa_prompt",
    )(dqb, dqib, p, p, kib, kvb, u)


def _lane_prefix(x):
    n = x.shape[1]
    lane = lax.broadcasted_iota(jnp.int32, x.shape, 1)
    s = 1
    while s < n:
        x = x + jnp.where(lane >= s, pltpu.roll(x, s, 1), 0.0)
        s *= 2
    return x


def _dsa_scores_kernel(pt_ref, qi_ref, seg_ref, *rest, n_grp):
    ki_refs = rest[0:n_grp]
    keys_ref, knew_ref = rest[n_grp:]
    j = pl.program_id(1)
    seg = seg_ref[...]
    wcols = _wi_cols(seg)
    qi = _stack_heads(qi_ref[...], DSA_IDX_HEADS, LANE)
    k_t = jnp.concatenate([r[...].astype(MXU_DTYPE) for r in ki_refs], axis=1)
    k_t = jnp.concatenate([k_t, jnp.zeros((LANE - DSA_IDX_DIM, k_t.shape[1]), MXU_DTYPE)], axis=0)
    keys = _sort_key(_index_scores(qi, wcols, k_t, SUB, keys_transposed=True))
    for g in range(n_grp):
        keys_ref[g] = keys[:, g * PAGE_SIZE:(g + 1) * PAGE_SIZE]

    @pl.when(j == pl.num_programs(1) - 1)
    def _():
        lane = lax.broadcasted_iota(jnp.int32, (LANE, LANE), 1)
        kn = jnp.where(lane < DSA_IDX_DIM, _new_block(seg), 0.0)
        tot = _index_scores(qi, wcols, kn, SUB)
        tok = lax.broadcasted_iota(jnp.int32, tot.shape, 0)
        col = lax.broadcasted_iota(jnp.int32, tot.shape, 1)
        knew_ref[...] = _sort_key(jnp.where(col <= tok, tot, -jnp.inf))


def _dsa_threshold_kernel(keys_ref, knew_ref, thr_ref, need_ref, *, topk):
    n_tiles, rb, _ = keys_ref.shape

    def count(pred, bound):
        bound_b = jnp.broadcast_to(bound, (rb, LANE))

        def body(c, acc):
            return acc + jnp.where(pred(keys_ref[c], bound_b), 1.0, 0.0)

        acc = lax.fori_loop(0, n_tiles, body, jnp.where(pred(knew_ref[...], bound_b), 1.0, 0.0),
                            unroll=math.gcd(n_tiles, 8))
        return jnp.sum(acc, axis=1, keepdims=True)

    thr = _kth_largest(lambda cand: count(lambda kk, bb: kk >= bb, cand), rb, topk)
    thr_ref[...] = thr
    need_ref[...] = topk - count(lambda kk, bb: kk > bb, thr)


def _dsa_attend_kernel(pt_ref, q_ref, kvnew_ref, gate_ref, keys_ref, knew_ref, thr_ref, need_ref, *rest, n_grp):
    kv_refs = rest[0:n_grp]
    o_ref, run_ref, m_ref, l_ref, acc_ref = rest[n_grp:]
    j = pl.program_id(1)

    @pl.when(j == 0)
    def _():
        run_ref[...] = jnp.zeros(run_ref.shape, f32)
        _softmax_init(m_ref, l_ref, acc_ref)

    def select(kk):
        thr = thr_ref[...]
        eq = (kk == thr) & (thr > KEY_NEG_INF)
        eqf = eq.astype(f32)
        incl = _lane_prefix(eqf)
        sel = (kk > thr) | (eq & (run_ref[...] + incl - eqf < need_ref[...]))
        run_ref[...] = run_ref[...] + incl[:, -1:]
        return jnp.concatenate([sel] * DSA_HEADS, axis=0)

    qs = _stack_heads(q_ref[...], DSA_HEADS, DSA_HD)
    sel = select(jnp.concatenate([keys_ref[g] for g in range(n_grp)], axis=1))
    k = jnp.concatenate([r[pl.ds(0, PAGE_SIZE, stride=2), :].astype(MXU_DTYPE) for r in kv_refs], axis=0)
    v = jnp.concatenate([r[pl.ds(1, PAGE_SIZE, stride=2), :].astype(MXU_DTYPE) for r in kv_refs], axis=0)
    _softmax_update(jnp.where(sel, _mm_nt(qs, k), -jnp.inf), v, m_ref, l_ref, acc_ref)

    @pl.when(j == pl.num_programs(1) - 1)
    def _():
        sel_n = select(knew_ref[...])
        kvn = _new_block(kvnew_ref[...]).astype(MXU_DTYPE)
        s_n = jnp.where(sel_n, _mm_nt(qs, kvn[:, 0:DSA_HD]), -jnp.inf)
        _softmax_update(s_n, kvn[:, DSA_HD:], m_ref, l_ref, acc_ref)
        _dsa_finish(acc_ref, l_ref, gate_ref[...], o_ref, SUB)


def _dsa_sample(page_table, cache_ki, cache_kv, layer, dqb3, dqib3, p3, topk):
    db, n_pages = page_table.shape
    n_grp = math.gcd(n_pages, PAGES_PER_STEP)
    n_steps = n_pages // n_grp
    n_rows = db * SUB
    rows = DSA_HEADS * SUB
    pt = page_table.reshape(-1)

    def page_spec(shape):
        def spec(g):
            return pl.BlockSpec((None, None) + shape,
                                lambda b, j, pt: (layer, pt[b * n_pages + j * n_grp + g], 0, 0))
        return [spec(g) for g in range(n_grp)]

    def row3(w, col):
        return pl.BlockSpec((None, SUB, w), lambda b, j, pt: (b, 0, col))

    keys_spec = pl.BlockSpec((n_grp, SUB, PAGE_SIZE), lambda b, j, pt: (j, b, 0))
    per_seq = lambda w: pl.BlockSpec((SUB, w), lambda b, j, pt: (b, 0))

    keys, knew = pl.pallas_call(
        functools.partial(_dsa_scores_kernel, n_grp=n_grp),
        grid_spec=pltpu.PrefetchScalarGridSpec(
            num_scalar_prefetch=1,
            grid=(db, n_steps),
            in_specs=[row3(DSA_IDX_HEADS * LANE, 0), row3(LANE, OFF_DKI // LANE)]
                     + page_spec((DSA_IDX_DIM, PAGE_SIZE)),
            out_specs=[keys_spec, per_seq(LANE)]),
        out_shape=[jax.ShapeDtypeStruct((n_pages, n_rows, PAGE_SIZE), jnp.int32),
                   jax.ShapeDtypeStruct((n_rows, LANE), jnp.int32)],
        compiler_params=_cparams("parallel", "arbitrary"),
        name="dsa_scores",
    )(pt, dqib3, p3, *([cache_ki] * n_grp))

    rb = min(n_rows, LANE)
    thr, need = pl.pallas_call(
        functools.partial(_dsa_threshold_kernel, topk=topk),
        grid=(n_rows // rb,),
        in_specs=[pl.BlockSpec((n_pages, rb, PAGE_SIZE), lambda i: (0, i, 0)),
                  pl.BlockSpec((rb, LANE), lambda i: (i, 0))],
        out_specs=[pl.BlockSpec((rb, 1), lambda i: (i, 0)), pl.BlockSpec((rb, 1), lambda i: (i, 0))],
        out_shape=[jax.ShapeDtypeStruct((n_rows, 1), jnp.int32), jax.ShapeDtypeStruct((n_rows, 1), f32)],
        compiler_params=_cparams("parallel"),
        name="dsa_threshold",
    )(keys, knew)

    return pl.pallas_call(
        functools.partial(_dsa_attend_kernel, n_grp=n_grp),
        grid_spec=pltpu.PrefetchScalarGridSpec(
            num_scalar_prefetch=1,
            grid=(db, n_steps),
            in_specs=[row3(DSA_HEADS * DSA_HD, 0), row3(2 * DSA_HD, OFF_DKV // (2 * DSA_HD)),
                      row3(BRANCH_W, OFF_DGATE // BRANCH_W), keys_spec, per_seq(LANE), per_seq(1), per_seq(1)]
                     + page_spec((2 * PAGE_SIZE, DSA_HD)),
            out_specs=pl.BlockSpec((None, SUB, BRANCH_W), lambda b, j, pt: (b, 0, 0)),
            scratch_shapes=[pltpu.VMEM((SUB, 1), f32),
                            pltpu.VMEM((rows, 1), f32), pltpu.VMEM((rows, 1), f32),
                            pltpu.VMEM((rows, DSA_HD), f32)]),
        out_shape=jax.ShapeDtypeStruct((db, SUB, BRANCH_W), f32),
        compiler_params=_cparams("parallel", "arbitrary"),
        name="dsa_attend",
    )(pt, dqb3, p3, p3, keys, knew, thr, need, *([cache_kv] * n_grp))


def _merge_kernel(x_ref, oa_ref, ob_ref, oc_ref, od_ref, mg_ref, wb_ref, wo_ref, fg_ref, y_ref, *, final):
    mixed = jnp.zeros(x_ref.shape, f32)
    for n, o_ref in enumerate((oa_ref, ob_ref, oc_ref, od_ref)):
        gate = jax.nn.sigmoid(mg_ref[:, n * D_MODEL:(n + 1) * D_MODEL])
        mixed = mixed + gate * _mm(o_ref[...], wb_ref[n])
    y = x_ref[...] + _mm(mixed, wo_ref[...])
    y_ref[...] = _rms(y, fg_ref[...]) if final else y


def _merge(x, oa, ob, oc, od, p, wb, wo, fg, final):
    n = x.shape[0]
    tm = min(n, 256)

    def row(w):
        return pl.BlockSpec((tm, w), lambda i: (i, 0))

    def const(a):
        return pl.BlockSpec(a.shape, lambda i: (0,) * a.ndim)

    return pl.pallas_call(
        functools.partial(_merge_kernel, final=final),
        grid=(n // tm,),
        in_specs=[row(D_MODEL), row(BRANCH_W), row(BRANCH_W), row(BRANCH_W), row(BRANCH_W),
                  pl.BlockSpec((tm, N_BRANCH * D_MODEL), lambda i: (i, OFF_MERGE)),
                  const(wb), const(wo), const(fg)],
        out_specs=row(D_MODEL),
        out_shape=jax.ShapeDtypeStruct((n, D_MODEL), f32),
        compiler_params=_cparams("parallel"),
        name="merge",
    )(x, oa, ob, oc, od, p, wb, wo, fg)


def _pad_rows(w, height):
    return jnp.pad(w, ((0, height - w.shape[0]), (0, 0)))


def _layout_w_in(w_t):
    def seg(i):
        return w_t[_IN_OFFS[i]:_IN_OFFS[i + 1]]

    dqi = seg(_DQI)
    dqi_pad = jnp.concatenate(
        [_pad_rows(dqi[h * DSA_IDX_DIM:(h + 1) * DSA_IDX_DIM], LANE) for h in range(DSA_IDX_HEADS)], axis=0)
    parts = [seg(_MERGE), seg(_GV), seg(_GGATE), seg(_LX), seg(_LGATE), seg(_CGATE), seg(_DQ), seg(_DGATE),
             dqi_pad, seg(_GQ), seg(_GK), seg(_CQ), seg(_CKV), seg(_DK), seg(_DV),
             _pad_rows(seg(_GLR), LANE), _pad_rows(seg(_KR), LANE),
             _pad_rows(jnp.concatenate([seg(_DKI), seg(_DWI)], axis=0), LANE)]
    return _pad_rows(jnp.concatenate(parts, axis=0), W_PAD).astype(MXU_DTYPE)


def _rope_tables(pos):
    half = MLA_ROPE // 2
    inv = ROPE_THETA ** (-jnp.arange(half, dtype=f32) / half)
    ang = pos.astype(f32)[:, None] * inv[None, :]
    lane = np.arange(LANE)
    sign = np.where((lane % MLA_ROPE) < half, -1.0, 1.0).astype(np.float32)
    return jnp.tile(jnp.cos(ang), (1, LANE // half)), jnp.tile(jnp.sin(ang), (1, LANE // half)) * sign


def _block_diag(w):
    eye = jnp.eye(LRU_BLOCKS, dtype=w.dtype)
    return jnp.einsum('nde,nm->ndme', w, eye).reshape(LRU_W, LRU_W).astype(MXU_DTYPE)


def kernel(x_prompt, x_sample, cache_mla, cache_dsa_kv, cache_dsa_kidx, state_gla, state_lru_h, state_lru_conv, page_table, ln_gain, w_in, gla_w_g2, gla_b_g, gla_norm, lru_conv_w, lru_conv_b, lru_w_r, lru_b_r, lru_w_i, lru_b_i, lru_lambda, mla_q_norm, mla_w_uq, mla_kv_norm, mla_w_uk, mla_w_uv, w_branch, w_out, final_gain):
    B, S, D = x_prompt.shape
    DB, T, _ = x_sample.shape
    depth = w_in.shape[0]
    n_pages = page_table.shape[1]
    past_len = n_pages * PAGE_SIZE
    topk_p = min(DSA_TOPK, S // 4)
    topk_s = min(DSA_TOPK, (past_len + T) // 4)
    assert D == D_MODEL and T <= SUB and S % 256 == 0 and S >= 512

    cos_p, sin_p = _rope_tables(jnp.arange(S))
    cos_s, sin_s = _rope_tables(past_len + jnp.arange(SUB))
    seqs_per_tile = min(DB * SUB, 256) // SUB
    cos_s = jnp.tile(cos_s, (seqs_per_tile, 1))
    sin_s = jnp.tile(sin_s, (seqs_per_tile, 1))
    w_in_t = jnp.transpose(w_in, (2, 0, 1))
    cache_mla_t = jnp.swapaxes(cache_mla, 2, 3)
    cache_ki_t = jnp.swapaxes(cache_dsa_kidx, 2, 3)
    cache_kv = cache_dsa_kv.reshape(cache_dsa_kv.shape[:2] + (2 * PAGE_SIZE, DSA_HD))

    xp = x_prompt.reshape(B * S, D)
    xs = jnp.pad(x_sample, ((0, 0), (0, SUB - T), (0, 0))).reshape(DB * SUB, D)
    fg = final_gain.reshape(1, D)
    outs = [[] for _ in range(12)]
    for l in range(depth):
        final = l == depth - 1
        w_l = _layout_w_in(w_in_t[:, l, :])
        gain = ln_gain[l].reshape(1, D)
        wg2 = jnp.pad(gla_w_g2[l], ((0, LANE - GLA_GATE_RANK), (0, 0))).astype(MXU_DTYPE)
        bg = gla_b_g[l].reshape(1, -1)
        gn = gla_norm[l].reshape(1, -1)
        cw = lru_conv_w[l]
        cb = lru_conv_b[l].reshape(1, -1)
        wr = _block_diag(lru_w_r[l])
        wi = _block_diag(lru_w_i[l])
        br = lru_b_r[l].reshape(1, -1)
        bi = lru_b_i[l].reshape(1, -1)
        lam = lru_lambda[l].reshape(1, -1)
        qn = mla_q_norm[l].reshape(1, -1)
        kvn = mla_kv_norm[l].reshape(1, -1)
        wuq3 = mla_w_uq[l].reshape(MLA_Q_LORA, MLA_HEADS, MLA_NOPE + MLA_ROPE)
        wuq = jnp.concatenate([wuq3[:, :, :MLA_NOPE].reshape(MLA_Q_LORA, -1),
                               wuq3[:, :, MLA_NOPE:].reshape(MLA_Q_LORA, -1)], axis=1).astype(MXU_DTYPE)
        wuk = mla_w_uk[l].reshape(MLA_KV_LORA, MLA_HEADS * MLA_NOPE).T.astype(MXU_DTYPE)
        wuv = jnp.transpose(mla_w_uv[l], (1, 0, 2)).astype(MXU_DTYPE)
        wb = w_branch[l].astype(MXU_DTYPE)
        wo = w_out[l].astype(MXU_DTYPE)

        p = _inproj(xp, gain, w_l)
        rows, mk, mq, kvf, kvb, kif, kib, dqb, dqib = _prep(p, cos_p, sin_p, qn, wuq, kvn, wuk)
        o_a, gla_s = _gla_prompt(p, B, S, wg2, bg, gn)
        o_b, lru_h, conv_new = _lru_prompt(p, B, S, cw, cb, wr, br, wi, bi, lam)
        o_c = _mla_prompt(mq, mk, p, B, S, wuv)
        o_d = _dsa_prompt(dqb, dqib, kib, kvb, p, B, S, topk_p)
        xp = _merge(xp, o_a, o_b, o_c, o_d, p, wb, wo, fg, final)
        outs[0].append(rows.reshape(B, S, MLA_ROW))
        outs[2].append(kvf.reshape(B, S, 2, DSA_KV_HEADS, DSA_HD))
        outs[4].append(kif.reshape(B, S, DSA_IDX_DIM))
        outs[6].append(gla_s)
        outs[8].append(lru_h.reshape(B, LRU_W))
        outs[10].append(conv_new)

        p = _inproj(xs, gain, w_l)
        rows, mk, mq, kvf, kvb, kif, kib, dqb, dqib = _prep(p, cos_s, sin_s, qn, wuq, kvn, wuk)
        p2 = p.reshape(DB, SUB * W_PAD)
        p3 = p.reshape(DB, SUB, W_PAD)
        o_a, gla_s = _gla_sample(p2, state_gla[l], T, wg2, bg, gn)
        o_b, lru_h, conv_new = _lru_sample(p2, state_lru_conv[l].reshape(DB, -1), state_lru_h[l], T,
                                           cw, cb, wr, br, wi, bi, lam)
        o_c = _mla_sample(page_table, cache_mla_t, l, mq.reshape(DB, SUB, -1), mk.reshape(DB, SUB, -1), p3, wuv)
        o_d = _dsa_sample(page_table, cache_ki_t, cache_kv, l, dqb.reshape(DB, SUB, -1),
                          dqib.reshape(DB, SUB, -1), p3, topk_s)
        xs = _merge(xs, o_a.reshape(DB * SUB, -1), o_b.reshape(DB * SUB, -1), o_c.reshape(DB * SUB, -1),
                    o_d.reshape(DB * SUB, -1), p, wb, wo, fg, final)
        outs[1].append(rows.reshape(DB, SUB, MLA_ROW)[:, :T])
        outs[3].append(kvf.reshape(DB, SUB, 2, DSA_KV_HEADS, DSA_HD)[:, :T])
        outs[5].append(kif.reshape(DB, SUB, DSA_IDX_DIM)[:, :T])
        outs[7].append(gla_s)
        outs[9].append(lru_h)
        outs[11].append(conv_new.reshape(DB, CONV_W - 1, LRU_W))

    y_prompt = xp.reshape(B, S, D)
    y_sample = xs.reshape(DB, SUB, D)[:, :T]
    return (y_prompt, y_sample) + tuple(jnp.stack(o) for o in outs)
```

```python
import functools
import math

import numpy as np
import jax
import jax.numpy as jnp
from jax import lax
from jax.experimental import pallas as pl
from jax.experimental.pallas import tpu as pltpu

f32 = jnp.float32
bf16 = jnp.bfloat16
MXU_DTYPE = bf16

LANE = 128
SUB = 8
VMEM_LIMIT = 56 * 1024 * 1024
PAGES_PER_STEP = 64

D_MODEL = 1024
PAGE_SIZE = 128
N_BRANCH = 4
BRANCH_W = D_MODEL // 2
GLA_HEADS = 4
GLA_DV = BRANCH_W // GLA_HEADS
GLA_DK = GLA_DV // 2
GLA_GATE_RANK = 16
GLA_TAU = 16.0
GLA_CHUNK = 64
LRU_W = BRANCH_W
LRU_BLOCKS = 8
LRU_BW = LRU_W // LRU_BLOCKS
LRU_C = 8.0
CONV_W = 4
MLA_HEADS = 4
MLA_NOPE = 64
MLA_ROPE = 32
MLA_V = BRANCH_W // MLA_HEADS
MLA_Q_LORA = D_MODEL // 4
MLA_KV_LORA = D_MODEL // 4
MLA_ROW = MLA_KV_LORA + MLA_ROPE
MLA_SCALE = (MLA_NOPE + MLA_ROPE) ** -0.5
MLA_KPAD = MLA_KV_LORA + LANE
ROPE_THETA = 10000.0
DSA_HEADS = 4
DSA_KV_HEADS = 1
DSA_HD = BRANCH_W // DSA_HEADS
DSA_IDX_HEADS = 4
DSA_IDX_DIM = 64
DSA_TOPK = 256
EPS = 1e-6
IN_SIZES = (
    GLA_HEADS * GLA_DK, GLA_HEADS * GLA_DK, GLA_HEADS * GLA_DV, GLA_GATE_RANK, BRANCH_W,
    LRU_W, LRU_W,
    MLA_Q_LORA, MLA_KV_LORA, MLA_ROPE, BRANCH_W,
    DSA_HEADS * DSA_HD, DSA_KV_HEADS * DSA_HD, DSA_KV_HEADS * DSA_HD,
    DSA_IDX_HEADS * DSA_IDX_DIM, DSA_IDX_DIM, DSA_IDX_HEADS, BRANCH_W,
    N_BRANCH * D_MODEL,
)
(_GQ, _GK, _GV, _GLR, _GGATE, _LX, _LGATE, _CQ, _CKV, _KR, _CGATE,
 _DQ, _DK, _DV, _DQI, _DKI, _DWI, _DGATE, _MERGE) = range(len(IN_SIZES))
_IN_OFFS = np.concatenate([[0], np.cumsum(IN_SIZES)]).tolist()

OFF_MERGE = 0
OFF_GV = 4096
OFF_GGATE = 4608
OFF_LX = 5120
OFF_LGATE = 5632
OFF_CGATE = 6144
OFF_DQ = 6656
OFF_DGATE = 7168
OFF_DQI = 7680
OFF_GQ = 8192
OFF_GK = 8448
OFF_CQ = 8704
OFF_CKV = 8960
OFF_DKV = 9216
OFF_GLR = 9472
OFF_KR = 9600
OFF_DKI = 9728
W_PAD = 10240

KEY_NEG_INF = -2139095041
INT_MIN = -2147483648

NT_DIMS = (((1,), (1,)), ((), ()))
TN_DIMS = (((0,), (0,)), ((), ()))


def _cparams(*sem):
    return pltpu.CompilerParams(dimension_semantics=sem, vmem_limit_bytes=VMEM_LIMIT)


def _mm(a, b):
    return jnp.dot(a.astype(MXU_DTYPE), b.astype(MXU_DTYPE), preferred_element_type=f32)


def _mm_nt(a, b):
    return lax.dot_general(a.astype(MXU_DTYPE), b.astype(MXU_DTYPE), NT_DIMS, preferred_element_type=f32)


def _mm_tn(a, b):
    return lax.dot_general(a.astype(MXU_DTYPE), b.astype(MXU_DTYPE), TN_DIMS, preferred_element_type=f32)


def _rms(x, g):
    return x * lax.rsqrt(jnp.mean(x * x, axis=-1, keepdims=True) + EPS) * g


def _silu(x):
    return x * jax.nn.sigmoid(x)


def _softplus(x):
    return jnp.maximum(x, 0.0) + jnp.log1p(jnp.exp(-jnp.abs(x)))


def _neg_expm1(y):
    acc = jnp.full_like(y, 1.0 / 479001600.0)
    for k in (39916800.0, 3628800.0, 362880.0, 40320.0, 5040.0, 720.0, 120.0, 24.0, 6.0, 2.0, 1.0):
        acc = acc * y + 1.0 / k
    return jnp.where(y > -0.25, -(acc * y), 1.0 - jnp.exp(y))


def _sort_key(x):
    b = lax.bitcast_convert_type(x, jnp.int32)
    return b ^ ((b >> 31) & jnp.int32(0x7FFFFFFF))


def _softmax_update(s, v, m_ref, l_ref, acc_ref, v_transposed=False):
    m_old = m_ref[...]
    m_new = jnp.maximum(m_old, jnp.max(s, axis=1, keepdims=True))
    m_safe = jnp.where(m_new == -jnp.inf, 0.0, m_new)
    alpha = jnp.exp(m_old - m_safe)
    p = jnp.exp(s - m_safe)
    l_ref[...] = alpha * l_ref[...] + jnp.sum(p, axis=1, keepdims=True)
    acc_ref[...] = alpha * acc_ref[...] + (_mm_nt(p, v) if v_transposed else _mm(p, v))
    m_ref[...] = m_new


def _softmax_init(m_ref, l_ref, acc_ref):
    m_ref[...] = jnp.full(m_ref.shape, -jnp.inf, f32)
    l_ref[...] = jnp.zeros(l_ref.shape, f32)
    acc_ref[...] = jnp.zeros(acc_ref.shape, f32)


def _stack_heads(x, n, w):
    return jnp.concatenate([x[:, h * w:(h + 1) * w] for h in range(n)], axis=0)


def _inproj_kernel(x_ref, g_ref, w_ref, o_ref, h_ref):
    @pl.when(pl.program_id(1) == 0)
    def _():
        h_ref[...] = _rms(x_ref[...], g_ref[...]).astype(h_ref.dtype)

    o_ref[...] = lax.dot_general(h_ref[...], w_ref[...], NT_DIMS, preferred_element_type=f32)


def _inproj(x, gain, w_t):
    n, d = x.shape
    tm = min(n, 1024)
    tn = 512
    return pl.pallas_call(
        _inproj_kernel,
        grid=(n // tm, W_PAD // tn),
        in_specs=[pl.BlockSpec((tm, d), lambda i, j: (i, 0)),
                  pl.BlockSpec((1, d), lambda i, j: (0, 0)),
                  pl.BlockSpec((tn, d), lambda i, j: (j, 0))],
        out_specs=pl.BlockSpec((tm, tn), lambda i, j: (i, j)),
        out_shape=jax.ShapeDtypeStruct((n, W_PAD), f32),
        scratch_shapes=[pltpu.VMEM((tm, d), MXU_DTYPE)],
        compiler_params=_cparams("parallel", "arbitrary"),
        name="inproj",
    )(x, gain, w_t)


def _prep_kernel(cq_ref, ckv_ref, kr_ref, dq_ref, dqi_ref, dkv_ref, dki_ref, cos_ref, sin_ref,
                 qn_ref, wuq_ref, kvn_ref, wuk_ref,
                 rows_ref, mk_ref, mq_ref, kvf_ref, kvb_ref, kif_ref, kib_ref, dqb_ref, dqib_ref):
    tm = cq_ref.shape[0]
    cos = cos_ref[...]
    sin = sin_ref[...]
    lane = lax.broadcasted_iota(jnp.int32, (tm, LANE), 1)
    first_half = (lane % MLA_ROPE) < (MLA_ROPE // 2)

    def rope(x):
        swapped = jnp.where(first_half, pltpu.roll(x, LANE - MLA_ROPE // 2, 1), pltpu.roll(x, MLA_ROPE // 2, 1))
        return x * cos + swapped * sin

    c = _rms(ckv_ref[...], kvn_ref[...])
    kr = rope(kr_ref[...])
    rows_ref[:, 0:MLA_KV_LORA] = c
    rows_ref[:, MLA_KV_LORA:MLA_ROW] = kr[:, 0:MLA_ROPE]
    mk_ref[:, 0:MLA_KV_LORA] = c.astype(mk_ref.dtype)
    mk_ref[:, MLA_KV_LORA:MLA_KPAD] = kr.astype(mk_ref.dtype)

    q = _mm(_rms(cq_ref[...], qn_ref[...]), wuq_ref[...])
    q_nope = q[:, 0:MLA_HEADS * MLA_NOPE]
    q_rope = rope(q[:, MLA_HEADS * MLA_NOPE:])
    lane_q = lax.broadcasted_iota(jnp.int32, q_nope.shape, 1)
    wuk = wuk_ref[...]
    for h in range(MLA_HEADS):
        q_h = jnp.where(lane_q // MLA_NOPE == h, q_nope, 0.0)
        q_lat = _mm(q_h, wuk)
        base = h * MLA_KPAD
        mq_ref[:, base:base + MLA_KV_LORA] = (q_lat * MLA_SCALE).astype(mq_ref.dtype)
        qr = q_rope if h == 0 else pltpu.roll(q_rope, LANE - h * MLA_ROPE, 1)
        qr = jnp.where(lane < MLA_ROPE, qr, 0.0)
        mq_ref[:, base + MLA_KV_LORA:base + MLA_KPAD] = (qr * MLA_SCALE).astype(mq_ref.dtype)

    dkv = dkv_ref[...]
    kvf_ref[...] = dkv
    kvb_ref[...] = dkv.astype(kvb_ref.dtype)
    seg = dki_ref[...]
    kif_ref[...] = seg[:, 0:DSA_IDX_DIM]
    kib_ref[...] = jnp.where(lane < DSA_IDX_DIM, seg, 0.0).astype(kib_ref.dtype)
    dqb_ref[...] = (dq_ref[...] * (DSA_HD ** -0.5)).astype(dqb_ref.dtype)
    dqib_ref[...] = (dqi_ref[...] * (DSA_IDX_DIM ** -0.5)).astype(dqib_ref.dtype)


def _prep(p, cos_t, sin_t, q_norm, w_uq, kv_norm, w_uk):
    n = p.shape[0]
    tm = min(n, 256)
    nt = cos_t.shape[0] // tm

    def seg(off, w):
        return pl.BlockSpec((tm, w), lambda i: (i, off // w))

    def const(a):
        return pl.BlockSpec(a.shape, lambda i: (0,) * a.ndim)

    def out(w):
        return pl.BlockSpec((tm, w), lambda i: (i, 0))

    outs = [(MLA_ROW, f32), (MLA_KPAD, MXU_DTYPE), (MLA_HEADS * MLA_KPAD, MXU_DTYPE),
            (2 * DSA_HD, f32), (2 * DSA_HD, MXU_DTYPE), (DSA_IDX_DIM, f32), (LANE, MXU_DTYPE),
            (DSA_HEADS * DSA_HD, MXU_DTYPE), (DSA_IDX_HEADS * LANE, MXU_DTYPE)]
    return pl.pallas_call(
        _prep_kernel,
        grid=(n // tm,),
        in_specs=[seg(OFF_CQ, 256), seg(OFF_CKV, 256), seg(OFF_KR, LANE), seg(OFF_DQ, 512),
                  seg(OFF_DQI, 512), seg(OFF_DKV, 256), seg(OFF_DKI, LANE),
                  pl.BlockSpec((tm, LANE), lambda i: (i % nt, 0)),
                  pl.BlockSpec((tm, LANE), lambda i: (i % nt, 0)),
                  const(q_norm), const(w_uq), const(kv_norm), const(w_uk)],
        out_specs=[out(w) for w, _ in outs],
        out_shape=[jax.ShapeDtypeStruct((n, w), dt) for w, dt in outs],
        compiler_params=_cparams("parallel"),
        name="prep",
    )(p, p, p, p, p, p, p, cos_t, sin_t, q_norm, w_uq, kv_norm, w_uk)


def _gla_gate(glr, wg2, bg):
    z = _mm(glr, wg2) + bg
    return (jnp.minimum(z, 0.0) - jnp.log1p(jnp.exp(-jnp.abs(z)))) * (1.0 / GLA_TAU)


def _gla_out(o, gate, gn):
    return _rms(o, gn) * _silu(gate)


def _gla_prompt_kernel(q_ref, k_ref, v_ref, gate_ref, glr_ref, wg2_ref, bg_ref, gn_ref,
                       o_ref, s_ref, st_ref, *, n_chunks):
    t = pl.program_id(1)

    @pl.when(t == 0)
    def _():
        st_ref[...] = jnp.zeros(st_ref.shape, f32)

    C = GLA_CHUNK
    HK = GLA_HEADS * GLA_DK
    r_i = lax.broadcasted_iota(jnp.int32, (C, C), 0)
    c_i = lax.broadcasted_iota(jnp.int32, (C, C), 1)
    causal = r_i >= c_i
    tril = causal.astype(f32)
    lane_head = lax.broadcasted_iota(jnp.int32, (C, HK), 1) // GLA_DK
    lane_head_st = lax.broadcasted_iota(jnp.int32, (GLA_DV, HK), 1) // GLA_DK
    causal_h = jnp.concatenate([causal] * GLA_HEADS, axis=0)
    gn = gn_ref[...]
    wg2 = wg2_ref[...]
    bg = bg_ref[...]
    for c in range(n_chunks):
        rows = slice(c * C, (c + 1) * C)
        g = _gla_gate(glr_ref[rows, :], wg2, bg)
        b = jnp.dot(tril, g, precision=lax.Precision.HIGHEST, preferred_element_type=f32)
        b_last = b[C - 1:C, :]
        q = q_ref[rows, :] * (GLA_DK ** -0.5)
        k = k_ref[rows, :]
        v = v_ref[rows, :]
        q_dec = q * jnp.exp(b)
        k_dec = k * jnp.exp(-b)
        k_tail = k * jnp.exp(b_last - b)
        decay = jnp.exp(b_last)
        q_heads = jnp.concatenate(
            [jnp.where(lane_head == h, q_dec, 0.0) for h in range(GLA_HEADS)], axis=0)
        att = jnp.where(causal_h, _mm_nt(q_heads, k_dec), 0.0)
        st = st_ref[...]
        o_inter = _mm_nt(q_heads, st)
        gate = gate_ref[rows, :]
        for h in range(GLA_HEADS):
            hv = slice(h * GLA_DV, (h + 1) * GLA_DV)
            o_h = _mm(att[h * C:(h + 1) * C, :], v[:, hv]) + o_inter[h * C:(h + 1) * C, :]
            o_ref[rows, hv] = _gla_out(o_h, gate[:, hv], gn)
        kv = _mm_tn(v, k_tail)
        upd = jnp.zeros((GLA_DV, HK), f32)
        for h in range(GLA_HEADS):
            upd = upd + jnp.where(lane_head_st == h, kv[h * GLA_DV:(h + 1) * GLA_DV, :], 0.0)
        st_ref[...] = decay * st + upd

    @pl.when(t == pl.num_programs(1) - 1)
    def _():
        s_ref[...] = st_ref[...].T.reshape(GLA_HEADS, GLA_DK, GLA_DV)


def _gla_prompt(p, B, L, wg2, bg, gn):
    tl = min(L, 256)
    nt = L // tl

    def seg(off, w):
        return pl.BlockSpec((tl, w), lambda b, t: (b * nt + t, off // w))

    def const(a):
        return pl.BlockSpec(a.shape, lambda b, t: (0,) * a.ndim)

    return pl.pallas_call(
        functools.partial(_gla_prompt_kernel, n_chunks=tl // GLA_CHUNK),
        grid=(B, nt),
        in_specs=[seg(OFF_GQ, 256), seg(OFF_GK, 256), seg(OFF_GV, 512), seg(OFF_GGATE, 512),
                  seg(OFF_GLR, LANE), const(wg2), const(bg), const(gn)],
        out_specs=[pl.BlockSpec((tl, BRANCH_W), lambda b, t: (b * nt + t, 0)),
                   pl.BlockSpec((None, GLA_HEADS, GLA_DK, GLA_DV), lambda b, t: (b, 0, 0, 0))],
        out_shape=[jax.ShapeDtypeStruct((B * L, BRANCH_W), f32),
                   jax.ShapeDtypeStruct((B, GLA_HEADS, GLA_DK, GLA_DV), f32)],
        scratch_shapes=[pltpu.VMEM((GLA_DV, GLA_HEADS * GLA_DK), f32)],
        compiler_params=_cparams("parallel", "arbitrary"),
        name="gla_prompt",
    )(p, p, p, p, p, wg2, bg, gn)


def _gla_sample_kernel(*refs, n_tok, bg_rows):
    q_refs = refs[0:n_tok]
    k_refs = refs[n_tok:2 * n_tok]
    v_refs = refs[2 * n_tok:3 * n_tok]
    gate_refs = refs[3 * n_tok:4 * n_tok]
    glr_refs = refs[4 * n_tok:5 * n_tok]
    s0_ref, wg2_ref, bg_ref, gn_ref, o_ref, s_ref, tr_ref, oraw_ref = refs[5 * n_tok:]
    HK = GLA_HEADS * GLA_DK
    zpad = jnp.zeros((LANE - bg_rows, HK), f32)

    def col_major(x):
        return jnp.concatenate([x, zpad], axis=0).T

    for t in range(n_tok):
        g = _gla_gate(glr_refs[t][...], wg2_ref[...], bg_ref[...])
        tr_ref[3 * t + 0] = col_major(q_refs[t][...] * (GLA_DK ** -0.5))
        tr_ref[3 * t + 1] = col_major(k_refs[t][...])
        tr_ref[3 * t + 2] = col_major(jnp.exp(g))
    for i in range(bg_rows):
        s = s0_ref[i].reshape(HK, GLA_DV)
        for t in range(n_tok):
            qc = tr_ref[3 * t + 0, :, i:i + 1]
            kc = tr_ref[3 * t + 1, :, i:i + 1]
            ec = tr_ref[3 * t + 2, :, i:i + 1]
            vrow = v_refs[t][i:i + 1, :]
            vb = jnp.concatenate(
                [jnp.broadcast_to(vrow[:, h * GLA_DV:(h + 1) * GLA_DV], (GLA_DK, GLA_DV))
                 for h in range(GLA_HEADS)], axis=0)
            s = ec * s + kc * vb
            prod = qc * s
            oraw_ref[t, i:i + 1, :] = jnp.concatenate(
                [jnp.sum(prod[h * GLA_DK:(h + 1) * GLA_DK, :], axis=0, keepdims=True)
                 for h in range(GLA_HEADS)], axis=1)
        s_ref[i] = s.reshape(GLA_HEADS, GLA_DK, GLA_DV)
    gn = gn_ref[...]
    for t in range(n_tok):
        o = oraw_ref[t]
        gate = gate_refs[t][...]
        for h in range(GLA_HEADS):
            hv = slice(h * GLA_DV, (h + 1) * GLA_DV)
            o_ref[:, t * BRANCH_W + h * GLA_DV:t * BRANCH_W + (h + 1) * GLA_DV] = _gla_out(o[:, hv], gate[:, hv], gn)
    o_ref[:, n_tok * BRANCH_W:] = jnp.zeros((bg_rows, (SUB - n_tok) * BRANCH_W), f32)


def _gla_sample(p2, s0, n_tok, wg2, bg, gn):
    db = p2.shape[0]
    bgr = min(db, 16)

    def seg(t, off, w):
        return pl.BlockSpec((bgr, w), lambda i: (i, (t * W_PAD + off) // w))

    def const(a):
        return pl.BlockSpec(a.shape, lambda i: (0,) * a.ndim)

    in_specs = ([seg(t, OFF_GQ, 256) for t in range(n_tok)] + [seg(t, OFF_GK, 256) for t in range(n_tok)]
                + [seg(t, OFF_GV, 512) for t in range(n_tok)] + [seg(t, OFF_GGATE, 512) for t in range(n_tok)]
                + [seg(t, OFF_GLR, LANE) for t in range(n_tok)]
                + [pl.BlockSpec((bgr, GLA_HEADS, GLA_DK, GLA_DV), lambda i: (i, 0, 0, 0)),
                   const(wg2), const(bg), const(gn)])
    return pl.pallas_call(
        functools.partial(_gla_sample_kernel, n_tok=n_tok, bg_rows=bgr),
        grid=(db // bgr,),
        in_specs=in_specs,
        out_specs=[pl.BlockSpec((bgr, SUB * BRANCH_W), lambda i: (i, 0)),
                   pl.BlockSpec((bgr, GLA_HEADS, GLA_DK, GLA_DV), lambda i: (i, 0, 0, 0))],
        out_shape=[jax.ShapeDtypeStruct((db, SUB * BRANCH_W), f32),
                   jax.ShapeDtypeStruct(s0.shape, f32)],
        scratch_shapes=[pltpu.VMEM((3 * n_tok, GLA_HEADS * GLA_DK, LANE), f32),
                        pltpu.VMEM((n_tok, bgr, BRANCH_W), f32)],
        compiler_params=_cparams("parallel"),
        name="gla_sample",
    )(*([p2] * (5 * n_tok)), s0, wg2, bg, gn)


def _lru_gates(xc, wr, br, wi, bi, sp):
    r = jax.nn.sigmoid(_mm(xc, wr) + br)
    ig = jax.nn.sigmoid(_mm(xc, wi) + bi)
    log_a = -LRU_C * r * sp
    return jnp.exp(log_a), jnp.sqrt(_neg_expm1(2.0 * log_a)) * (ig * xc)


def _lru_prompt_kernel(x_ref, gate_ref, cw_ref, cb_ref, wr_ref, br_ref, wi_ref, bi_ref, lam_ref,
                       o_ref, hfin_ref, conv_ref, tail_ref, h_ref, a_ref, u_ref, hs_ref):
    t = pl.program_id(1)
    tl = x_ref.shape[0]

    @pl.when(t == 0)
    def _():
        tail_ref[...] = jnp.zeros(tail_ref.shape, f32)
        h_ref[...] = jnp.zeros(h_ref.shape, f32)

    x = x_ref[...]
    ext = jnp.concatenate([tail_ref[...], x], axis=0)
    cw = cw_ref[...]
    xc = jnp.zeros(x.shape, f32)
    for j in range(CONV_W):
        s = SUB - (CONV_W - 1) + j
        xc = xc + ext[s:s + tl, :] * cw[j:j + 1, :]
    xc = cb_ref[...] + xc
    a, u = _lru_gates(xc, wr_ref[...], br_ref[...], wi_ref[...], bi_ref[...], _softplus(-lam_ref[...]))
    a_ref[...] = a
    u_ref[...] = u

    def step(i, h):
        h = a_ref[pl.ds(i, 1), :] * h + u_ref[pl.ds(i, 1), :]
        hs_ref[pl.ds(i, 1), :] = h
        return h

    h = lax.fori_loop(0, tl, step, h_ref[...], unroll=8)
    h_ref[...] = h
    tail_ref[...] = x[tl - SUB:, :]
    o_ref[...] = hs_ref[...] * _silu(gate_ref[...])

    @pl.when(t == pl.num_programs(1) - 1)
    def _():
        hfin_ref[...] = h
        conv_ref[...] = x[tl - (CONV_W - 1):, :]


def _lru_prompt(p, B, L, cw, cb, wr, br, wi, bi, lam):
    tl = min(L, 256)
    nt = L // tl

    def seg(off, w):
        return pl.BlockSpec((tl, w), lambda b, t: (b * nt + t, off // w))

    def const(a):
        return pl.BlockSpec(a.shape, lambda b, t: (0,) * a.ndim)

    return pl.pallas_call(
        _lru_prompt_kernel,
        grid=(B, nt),
        in_specs=[seg(OFF_LX, 512), seg(OFF_LGATE, 512), const(cw), const(cb), const(wr), const(br),
                  const(wi), const(bi), const(lam)],
        out_specs=[pl.BlockSpec((tl, LRU_W), lambda b, t: (b * nt + t, 0)),
                   pl.BlockSpec((None, 1, LRU_W), lambda b, t: (b, 0, 0)),
                   pl.BlockSpec((None, CONV_W - 1, LRU_W), lambda b, t: (b, 0, 0))],
        out_shape=[jax.ShapeDtypeStruct((B * L, LRU_W), f32),
                   jax.ShapeDtypeStruct((B, 1, LRU_W), f32),
                   jax.ShapeDtypeStruct((B, CONV_W - 1, LRU_W), f32)],
        scratch_shapes=[pltpu.VMEM((SUB, LRU_W), f32), pltpu.VMEM((1, LRU_W), f32),
                        pltpu.VMEM((tl, LRU_W), f32), pltpu.VMEM((tl, LRU_W), f32),
                        pltpu.VMEM((tl, LRU_W), f32)],
        compiler_params=_cparams("parallel", "arbitrary"),
        name="lru_prompt",
    )(p, p, cw, cb, wr, br, wi, bi, lam)


def _lru_sample_kernel(*refs, n_tok):
    x_refs = refs[0:n_tok]
    gate_refs = refs[n_tok:2 * n_tok]
    (buf_ref, h0_ref, cw_ref, cb_ref, wr_ref, br_ref, wi_ref, bi_ref, lam_ref,
     o_ref, hfin_ref, conv_ref) = refs[2 * n_tok:]
    W = LRU_W
    xs = [buf_ref[:, j * W:(j + 1) * W] for j in range(CONV_W - 1)] + [r[...] for r in x_refs]
    cw = cw_ref[...]
    sp = _softplus(-lam_ref[...])
    h = h0_ref[...]
    for t in range(n_tok):
        xc = jnp.zeros(h.shape, f32)
        for j in range(CONV_W):
            xc = xc + xs[t + j] * cw[j:j + 1, :]
        xc = cb_ref[...] + xc
        a, u = _lru_gates(xc, wr_ref[...], br_ref[...], wi_ref[...], bi_ref[...], sp)
        h = a * h + u
        o_ref[:, t * W:(t + 1) * W] = h * _silu(gate_refs[t][...])
    o_ref[:, n_tok * W:] = jnp.zeros((h.shape[0], (SUB - n_tok) * W), f32)
    hfin_ref[...] = h
    for j in range(CONV_W - 1):
        conv_ref[:, j * W:(j + 1) * W] = xs[n_tok + j]


def _lru_sample(p2, buf, h0, n_tok, cw, cb, wr, br, wi, bi, lam):
    db = p2.shape[0]

    def seg(t, off, w):
        return pl.BlockSpec((db, w), lambda i: (0, (t * W_PAD + off) // w))

    def const(a):
        return pl.BlockSpec(a.shape, lambda i: (0,) * a.ndim)

    in_specs = ([seg(t, OFF_LX, 512) for t in range(n_tok)] + [seg(t, OFF_LGATE, 512) for t in range(n_tok)]
                + [const(a) for a in (buf, h0, cw, cb, wr, br, wi, bi, lam)])
    return pl.pallas_call(
        functools.partial(_lru_sample_kernel, n_tok=n_tok),
        grid=(1,),
        in_specs=in_specs,
        out_specs=[pl.BlockSpec((db, SUB * LRU_W), lambda i: (0, 0)),
                   pl.BlockSpec((db, LRU_W), lambda i: (0, 0)),
                   pl.BlockSpec((db, (CONV_W - 1) * LRU_W), lambda i: (0, 0))],
        out_shape=[jax.ShapeDtypeStruct((db, SUB * LRU_W), f32),
                   jax.ShapeDtypeStruct((db, LRU_W), f32),
                   jax.ShapeDtypeStruct((db, (CONV_W - 1) * LRU_W), f32)],
        compiler_params=_cparams("arbitrary"),
        name="lru_sample",
    )(*([p2] * (2 * n_tok)), buf, h0, cw, cb, wr, br, wi, bi, lam)


def _mla_finish(acc_ref, l_ref, wuv_ref, gate, o_ref, rows):
    o = acc_ref[...] / l_ref[...]
    for h in range(MLA_HEADS):
        oc = _mm(o[h * rows:(h + 1) * rows, :], wuv_ref[h])
        o_ref[:, h * MLA_V:(h + 1) * MLA_V] = oc * _silu(gate[:, h * MLA_V:(h + 1) * MLA_V])


def _mla_prompt_kernel(q_ref, k_ref, wuv_ref, gate_ref, o_ref, m_ref, l_ref, acc_ref, *, tq, tk):
    i = pl.program_id(1)
    qs = _stack_heads(q_ref[...], MLA_HEADS, MLA_KPAD)
    _softmax_init(m_ref, l_ref, acc_ref)
    n_full = (i * tq) // tk

    def keys(j):
        return k_ref[pl.ds(pl.multiple_of(j * tk, tk), tk), :]

    def body(j, s):
        s_next = _mm_nt(qs, keys(j + 1))
        _softmax_update(s, keys(j)[:, 0:MLA_KV_LORA], m_ref, l_ref, acc_ref)
        return s_next

    s = lax.fori_loop(0, n_full, body, _mm_nt(qs, keys(0)))
    row = i * tq + lax.broadcasted_iota(jnp.int32, s.shape, 0) % tq
    col = n_full * tk + lax.broadcasted_iota(jnp.int32, s.shape, 1)
    _softmax_update(jnp.where(col <= row, s, -jnp.inf), keys(n_full)[:, 0:MLA_KV_LORA], m_ref, l_ref, acc_ref)
    _mla_finish(acc_ref, l_ref, wuv_ref, gate_ref[...], o_ref, tq)


def _mla_prompt(mq, mk, p, B, L, wuv):
    tq = 128
    tk = min(L, 512)
    nq = L // tq
    rows = MLA_HEADS * tq
    return pl.pallas_call(
        functools.partial(_mla_prompt_kernel, tq=tq, tk=tk),
        grid=(B, nq),
        in_specs=[pl.BlockSpec((tq, MLA_HEADS * MLA_KPAD), lambda b, i: (b * nq + i, 0)),
                  pl.BlockSpec((L, MLA_KPAD), lambda b, i: (b, 0)),
                  pl.BlockSpec(wuv.shape, lambda b, i: (0, 0, 0)),
                  pl.BlockSpec((tq, BRANCH_W), lambda b, i: (b * nq + i, OFF_CGATE // BRANCH_W))],
        out_specs=pl.BlockSpec((tq, BRANCH_W), lambda b, i: (b * nq + i, 0)),
        out_shape=jax.ShapeDtypeStruct((B * L, BRANCH_W), f32),
        scratch_shapes=[pltpu.VMEM((rows, 1), f32), pltpu.VMEM((rows, 1), f32),
                        pltpu.VMEM((rows, MLA_KV_LORA), f32)],
        compiler_params=_cparams("parallel", "arbitrary"),
        name="mla_prompt",
    )(mq, mk, wuv, p)


def _new_block(x):
    return jnp.concatenate([x.astype(f32), jnp.zeros((LANE - SUB, x.shape[1]), f32)], axis=0)


def _mla_sample_kernel(pt_ref, q_ref, knew_ref, wuv_ref, gate_ref, *rest, n_grp):
    page_refs = rest[0:n_grp]
    o_ref, m_ref, l_ref, acc_ref = rest[n_grp:]
    j = pl.program_id(1)

    @pl.when(j == 0)
    def _():
        _softmax_init(m_ref, l_ref, acc_ref)

    qs = _stack_heads(q_ref[...], MLA_HEADS, MLA_KPAD)
    k_t = jnp.concatenate([r[...].astype(MXU_DTYPE) for r in page_refs], axis=1)
    k_t_pad = jnp.concatenate([k_t, jnp.zeros((MLA_KPAD - MLA_ROW, k_t.shape[1]), MXU_DTYPE)], axis=0)
    _softmax_update(_mm(qs, k_t_pad), k_t[0:MLA_KV_LORA, :], m_ref, l_ref, acc_ref, v_transposed=True)

    @pl.when(j == pl.num_programs(1) - 1)
    def _():
        kn = _new_block(knew_ref[...]).astype(MXU_DTYPE)
        s = _mm_nt(qs, kn)
        tok = lax.broadcasted_iota(jnp.int32, s.shape, 0) % SUB
        col = lax.broadcasted_iota(jnp.int32, s.shape, 1)
        _softmax_update(jnp.where(col <= tok, s, -jnp.inf), kn[:, 0:MLA_KV_LORA], m_ref, l_ref, acc_ref)
        _mla_finish(acc_ref, l_ref, wuv_ref, gate_ref[...], o_ref, SUB)


def _mla_sample(page_table, cache, layer, mq3, mk3, p3, wuv):
    db, n_pages = page_table.shape
    n_grp = math.gcd(n_pages, PAGES_PER_STEP)
    n_steps = n_pages // n_grp
    rows = MLA_HEADS * SUB

    def page_spec(g):
        return pl.BlockSpec((None, None, MLA_ROW, PAGE_SIZE),
                            lambda b, j, pt: (layer, pt[b * n_pages + j * n_grp + g], 0, 0))

    grid_spec = pltpu.PrefetchScalarGridSpec(
        num_scalar_prefetch=1,
        grid=(db, n_steps),
        in_specs=[pl.BlockSpec((None, SUB, MLA_HEADS * MLA_KPAD), lambda b, j, pt: (b, 0, 0)),
                  pl.BlockSpec((None, SUB, MLA_KPAD), lambda b, j, pt: (b, 0, 0)),
                  pl.BlockSpec(wuv.shape, lambda b, j, pt: (0, 0, 0)),
                  pl.BlockSpec((None, SUB, BRANCH_W), lambda b, j, pt: (b, 0, OFF_CGATE // BRANCH_W))]
                 + [page_spec(g) for g in range(n_grp)],
        out_specs=pl.BlockSpec((None, SUB, BRANCH_W), lambda b, j, pt: (b, 0, 0)),
        scratch_shapes=[pltpu.VMEM((rows, 1), f32), pltpu.VMEM((rows, 1), f32),
                        pltpu.VMEM((rows, MLA_KV_LORA), f32)])
    return pl.pallas_call(
        functools.partial(_mla_sample_kernel, n_grp=n_grp),
        grid_spec=grid_spec,
        out_shape=jax.ShapeDtypeStruct((db, SUB, BRANCH_W), f32),
        compiler_params=_cparams("parallel", "arbitrary"),
        name="mla_sample",
    )(page_table.reshape(-1), mq3, mk3, wuv, p3, *([cache] * n_grp))


def _index_scores(qi_stack, wcols, keys_b, rows, keys_transposed=False):
    return _weighted_relu(_mm(qi_stack, keys_b) if keys_transposed else _mm_nt(qi_stack, keys_b), wcols, rows)


def _weighted_relu(raw, wcols, rows):
    sc = jnp.maximum(raw, 0.0)
    tot = sc[0:rows, :] * wcols[0]
    for h in range(1, DSA_IDX_HEADS):
        tot = tot + sc[h * rows:(h + 1) * rows, :] * wcols[h]
    return jnp.where(tot == 0.0, 0.0, tot)


def _wi_cols(seg):
    scale = DSA_IDX_HEADS ** -0.5
    return [seg[:, DSA_IDX_DIM + h:DSA_IDX_DIM + h + 1] * scale for h in range(DSA_IDX_HEADS)]


def _kth_largest(count_ge, rows, topk):
    t0 = jnp.where(count_ge(jnp.zeros((rows, 1), jnp.int32)) >= topk,
                   jnp.int32(0), jnp.int32(INT_MIN))

    def bit_body(it, t):
        cand = t + lax.shift_left(jnp.int32(1), jnp.int32(30) - it)
        return jnp.where(count_ge(cand) >= topk, cand, t)

    return lax.fori_loop(0, 31, bit_body, t0)


def _dsa_finish(acc_ref, l_ref, gate, o_ref, rows):
    o = acc_ref[...] / l_ref[...]
    for h in range(DSA_HEADS):
        hv = slice(h * DSA_HD, (h + 1) * DSA_HD)
        o_ref[:, hv] = o[h * rows:(h + 1) * rows, :] * _silu(gate[:, hv])


def _dsa_prompt_kernel(q_ref, qi_ref, wi_ref, gate_ref, kib_ref, kvb_ref, u_ref, o_ref,
                       keys_ref, m_ref, l_ref, acc_ref, *, tq, tk, topk):
    i = pl.program_id(1)
    n_ch = (i * tq) // tk + 1
    qi = _stack_heads(qi_ref[...], DSA_IDX_HEADS, LANE)
    wcols = _wi_cols(wi_ref[...])
    row_pos = i * tq + lax.broadcasted_iota(jnp.int32, (tq, tk), 0)
    col_in = lax.broadcasted_iota(jnp.int32, (tq, tk), 1)

    def score_body(j, carry):
        kb = kib_ref[pl.ds(pl.multiple_of(j * tk, tk), tk), :]
        tot = _index_scores(qi, wcols, kb, tq)
        keys_ref[j] = _sort_key(jnp.where(j * tk + col_in <= row_pos, tot, -jnp.inf))
        return carry

    lax.fori_loop(0, n_ch, score_body, 0)

    def count(pred, bound):
        bound_b = jnp.broadcast_to(bound, (tq, LANE))

        def body(j, acc):
            for c in range(tk // LANE):
                acc = acc + jnp.where(pred(keys_ref[j, :, c * LANE:(c + 1) * LANE], bound_b), 1.0, 0.0)
            return acc

        acc = lax.fori_loop(0, n_ch, body, jnp.zeros((tq, LANE), f32))
        return jnp.sum(acc, axis=1, keepdims=True)

    thr = _kth_largest(lambda cand: count(lambda kk, bb: kk >= bb, cand), tq, topk)
    need = topk - count(lambda kk, bb: kk > bb, thr)
    tie_ok = thr > KEY_NEG_INF
    n_eq = count(lambda kk, bb: kk == bb, thr)
    ties_matter = jnp.max(jnp.where(tie_ok & (n_eq > need), 1.0, 0.0)) > 0.5
    qs = _stack_heads(q_ref[...], DSA_HEADS, DSA_HD)
    _softmax_init(m_ref, l_ref, acc_ref)

    def kv_block(j):
        return kvb_ref[pl.ds(pl.multiple_of(j * tk, tk), tk), :]

    def qk(j):
        return _mm_nt(qs, kv_block(j)[:, 0:DSA_HD])

    def attend(by_position):
        def consume(j, s, run):
            kk = keys_ref[j]
            eq = (kk == thr) & tie_ok
            if by_position:
                eqf = eq.astype(f32)
                before = _mm(eqf, u_ref[...]) + run
                sel = (kk > thr) | (eq & (before < need))
                run = run + jnp.sum(eqf, axis=1, keepdims=True)
            else:
                sel = (kk > thr) | eq
            s = jnp.where(jnp.concatenate([sel] * DSA_HEADS, axis=0), s, -jnp.inf)
            _softmax_update(s, kv_block(j)[:, DSA_HD:], m_ref, l_ref, acc_ref)
            return run

        def att_body(j, carry):
            s, run = carry
            s_next = qk(j + 1)
            return s_next, consume(j, s, run)

        s, run = lax.fori_loop(0, n_ch - 1, att_body, (qk(0), jnp.zeros((tq, 1), f32)))
        consume(n_ch - 1, s, run)

    @pl.when(ties_matter)
    def _():
        attend(True)

    @pl.when(jnp.logical_not(ties_matter))
    def _():
        attend(False)

    _dsa_finish(acc_ref, l_ref, gate_ref[...], o_ref, tq)


def _dsa_prompt(dqb, dqib, kib, kvb, p, B, L, topk):
    tq = 128
    tk = min(L, 512)
    nq = L // tq
    rows = DSA_HEADS * tq
    upper = (np.arange(tk)[:, None] < np.arange(tk)[None, :]).astype(np.float32)
    u = jnp.asarray(upper, MXU_DTYPE)
    return pl.pallas_call(
        functools.partial(_dsa_prompt_kernel, tq=tq, tk=tk, topk=topk),
        grid=(B, nq),
        in_specs=[pl.BlockSpec((tq, DSA_HEADS * DSA_HD), lambda b, i: (b * nq + i, 0)),
                  pl.BlockSpec((tq, DSA_IDX_HEADS * LANE), lambda b, i: (b * nq + i, 0)),
                  pl.BlockSpec((tq, LANE), lambda b, i: (b * nq + i, OFF_DKI // LANE)),
                  pl.BlockSpec((tq, BRANCH_W), lambda b, i: (b * nq + i, OFF_DGATE // BRANCH_W)),
                  pl.BlockSpec((L, LANE), lambda b, i: (b, 0)),
                  pl.BlockSpec((L, 2 * DSA_HD), lambda b, i: (b, 0)),
                  pl.BlockSpec((tk, tk), lambda b, i: (0, 0))],
        out_specs=pl.BlockSpec((tq, BRANCH_W), lambda b, i: (b * nq + i, 0)),
        out_shape=jax.ShapeDtypeStruct((B * L, BRANCH_W), f32),
        scratch_shapes=[pltpu.VMEM((L // tk, tq, tk), jnp.int32),
                        pltpu.VMEM((rows, 1), f32), pltpu.VMEM((rows, 1), f32),
                        pltpu.VMEM((rows, DSA_HD), f32)],
        compiler_params=_cparams("parallel", "arbitrary"),
        name="dsa_prompt",
    )(dqb, dqib, p, p, kib, kvb, u)


def _lane_prefix(x):
    n = x.shape[1]
    lane = lax.broadcasted_iota(jnp.int32, x.shape, 1)
    s = 1
    while s < n:
        x = x + jnp.where(lane >= s, pltpu.roll(x, s, 1), 0.0)
        s *= 2
    return x


def _dsa_scores_kernel(pt_ref, qi_ref, seg_ref, *rest, n_grp):
    ki_refs = rest[0:n_grp]
    keys_ref, knew_ref = rest[n_grp:]
    j = pl.program_id(1)
    seg = seg_ref[...]
    wcols = _wi_cols(seg)
    qi = _stack_heads(qi_ref[...], DSA_IDX_HEADS, LANE)
    k_t = jnp.concatenate([r[...].astype(MXU_DTYPE) for r in ki_refs], axis=1)
    k_t = jnp.concatenate([k_t, jnp.zeros((LANE - DSA_IDX_DIM, k_t.shape[1]), MXU_DTYPE)], axis=0)
    keys = _sort_key(_index_scores(qi, wcols, k_t, SUB, keys_transposed=True))
    for g in range(n_grp):
        keys_ref[g] = keys[:, g * PAGE_SIZE:(g + 1) * PAGE_SIZE]

    @pl.when(j == pl.num_programs(1) - 1)
    def _():
        lane = lax.broadcasted_iota(jnp.int32, (LANE, LANE), 1)
        kn = jnp.where(lane < DSA_IDX_DIM, _new_block(seg), 0.0)
        tot = _index_scores(qi, wcols, kn, SUB)
        tok = lax.broadcasted_iota(jnp.int32, tot.shape, 0)
        col = lax.broadcasted_iota(jnp.int32, tot.shape, 1)
        knew_ref[...] = _sort_key(jnp.where(col <= tok, tot, -jnp.inf))


def _dsa_threshold_kernel(keys_ref, knew_ref, thr_ref, need_ref, ties_ref, *, topk):
    n_tiles, rb, _ = keys_ref.shape

    def count(pred, bound):
        bound_b = jnp.broadcast_to(bound, (rb, LANE))

        def body(c, acc):
            return acc + jnp.where(pred(keys_ref[c], bound_b), 1.0, 0.0)

        acc = lax.fori_loop(0, n_tiles, body, jnp.where(pred(knew_ref[...], bound_b), 1.0, 0.0),
                            unroll=math.gcd(n_tiles, 8))
        return jnp.sum(acc, axis=1, keepdims=True)

    thr = _kth_largest(lambda cand: count(lambda kk, bb: kk >= bb, cand), rb, topk)
    need = topk - count(lambda kk, bb: kk > bb, thr)
    n_eq = count(lambda kk, bb: kk == bb, thr)
    thr_ref[...] = thr
    need_ref[...] = need
    ties_ref[...] = jnp.where((thr > KEY_NEG_INF) & (n_eq > need), 1.0, 0.0)


def _dsa_attend_kernel(pt_ref, q_ref, kvnew_ref, gate_ref, keys_ref, knew_ref, thr_ref, need_ref, ties_ref,
                       *rest, n_grp):
    kv_refs = rest[0:n_grp]
    o_ref, run_ref, m_ref, l_ref, acc_ref = rest[n_grp:]
    j = pl.program_id(1)

    @pl.when(j == 0)
    def _():
        run_ref[...] = jnp.zeros(run_ref.shape, f32)
        _softmax_init(m_ref, l_ref, acc_ref)

    ties_matter = jnp.max(ties_ref[...]) > 0.5

    def select(kk):
        thr = thr_ref[...]
        eq = (kk == thr) & (thr > KEY_NEG_INF)

        def by_position():
            eqf = eq.astype(f32)
            incl = _lane_prefix(eqf)
            picked = eq & (run_ref[...] + incl - eqf < need_ref[...])
            run_ref[...] = run_ref[...] + incl[:, -1:]
            return jnp.where((kk > thr) | picked, 1.0, 0.0)

        sel = lax.cond(ties_matter, by_position, lambda: jnp.where((kk > thr) | eq, 1.0, 0.0))
        return jnp.concatenate([sel] * DSA_HEADS, axis=0) > 0.5

    qs = _stack_heads(q_ref[...], DSA_HEADS, DSA_HD)
    sel = select(jnp.concatenate([keys_ref[g] for g in range(n_grp)], axis=1))
    k = jnp.concatenate([r[pl.ds(0, PAGE_SIZE, stride=2), :].astype(MXU_DTYPE) for r in kv_refs], axis=0)
    v = jnp.concatenate([r[pl.ds(1, PAGE_SIZE, stride=2), :].astype(MXU_DTYPE) for r in kv_refs], axis=0)
    _softmax_update(jnp.where(sel, _mm_nt(qs, k), -jnp.inf), v, m_ref, l_ref, acc_ref)

    @pl.when(j == pl.num_programs(1) - 1)
    def _():
        sel_n = select(knew_ref[...])
        kvn = _new_block(kvnew_ref[...]).astype(MXU_DTYPE)
        s_n = jnp.where(sel_n, _mm_nt(qs, kvn[:, 0:DSA_HD]), -jnp.inf)
        _softmax_update(s_n, kvn[:, DSA_HD:], m_ref, l_ref, acc_ref)
        _dsa_finish(acc_ref, l_ref, gate_ref[...], o_ref, SUB)


def _dsa_sample(page_table, cache_ki, cache_kv, layer, dqb3, dqib3, p3, topk):
    db, n_pages = page_table.shape
    n_grp = math.gcd(n_pages, PAGES_PER_STEP)
    n_steps = n_pages // n_grp
    n_rows = db * SUB
    rows = DSA_HEADS * SUB
    pt = page_table.reshape(-1)

    def page_spec(shape):
        def spec(g):
            return pl.BlockSpec((None, None) + shape,
                                lambda b, j, pt: (layer, pt[b * n_pages + j * n_grp + g], 0, 0))
        return [spec(g) for g in range(n_grp)]

    def row3(w, col):
        return pl.BlockSpec((None, SUB, w), lambda b, j, pt: (b, 0, col))

    keys_spec = pl.BlockSpec((n_grp, SUB, PAGE_SIZE), lambda b, j, pt: (j, b, 0))
    per_seq = lambda w: pl.BlockSpec((SUB, w), lambda b, j, pt: (b, 0))

    keys, knew = pl.pallas_call(
        functools.partial(_dsa_scores_kernel, n_grp=n_grp),
        grid_spec=pltpu.PrefetchScalarGridSpec(
            num_scalar_prefetch=1,
            grid=(db, n_steps),
            in_specs=[row3(DSA_IDX_HEADS * LANE, 0), row3(LANE, OFF_DKI // LANE)]
                     + page_spec((DSA_IDX_DIM, PAGE_SIZE)),
            out_specs=[keys_spec, per_seq(LANE)]),
        out_shape=[jax.ShapeDtypeStruct((n_pages, n_rows, PAGE_SIZE), jnp.int32),
                   jax.ShapeDtypeStruct((n_rows, LANE), jnp.int32)],
        compiler_params=_cparams("parallel", "arbitrary"),
        name="dsa_scores",
    )(pt, dqib3, p3, *([cache_ki] * n_grp))

    rb = min(n_rows, LANE)
    thr, need, ties = pl.pallas_call(
        functools.partial(_dsa_threshold_kernel, topk=topk),
        grid=(n_rows // rb,),
        in_specs=[pl.BlockSpec((n_pages, rb, PAGE_SIZE), lambda i: (0, i, 0)),
                  pl.BlockSpec((rb, LANE), lambda i: (i, 0))],
        out_specs=[pl.BlockSpec((rb, 1), lambda i: (i, 0))] * 3,
        out_shape=[jax.ShapeDtypeStruct((n_rows, 1), jnp.int32), jax.ShapeDtypeStruct((n_rows, 1), f32),
                   jax.ShapeDtypeStruct((n_rows, 1), f32)],
        compiler_params=_cparams("parallel"),
        name="dsa_threshold",
    )(keys, knew)

    return pl.pallas_call(
        functools.partial(_dsa_attend_kernel, n_grp=n_grp),
        grid_spec=pltpu.PrefetchScalarGridSpec(
            num_scalar_prefetch=1,
            grid=(db, n_steps),
            in_specs=[row3(DSA_HEADS * DSA_HD, 0), row3(2 * DSA_HD, OFF_DKV // (2 * DSA_HD)),
                      row3(BRANCH_W, OFF_DGATE // BRANCH_W), keys_spec, per_seq(LANE), per_seq(1), per_seq(1), per_seq(1)]
                     + page_spec((2 * PAGE_SIZE, DSA_HD)),
            out_specs=pl.BlockSpec((None, SUB, BRANCH_W), lambda b, j, pt: (b, 0, 0)),
            scratch_shapes=[pltpu.VMEM((SUB, 1), f32),
                            pltpu.VMEM((rows, 1), f32), pltpu.VMEM((rows, 1), f32),
                            pltpu.VMEM((rows, DSA_HD), f32)]),
        out_shape=jax.ShapeDtypeStruct((db, SUB, BRANCH_W), f32),
        compiler_params=_cparams("parallel", "arbitrary"),
        name="dsa_attend",
    )(pt, dqb3, p3, p3, keys, knew, thr, need, ties, *([cache_kv] * n_grp))


def _merge_kernel(x_ref, oa_ref, ob_ref, oc_ref, od_ref, mg_ref, wb_ref, wo_ref, fg_ref, y_ref, *, final):
    mixed = jnp.zeros(x_ref.shape, f32)
    for n, o_ref in enumerate((oa_ref, ob_ref, oc_ref, od_ref)):
        gate = jax.nn.sigmoid(mg_ref[:, n * D_MODEL:(n + 1) * D_MODEL])
        mixed = mixed + gate * _mm(o_ref[...], wb_ref[n])
    y = x_ref[...] + _mm(mixed, wo_ref[...])
    y_ref[...] = _rms(y, fg_ref[...]) if final else y


def _merge(x, oa, ob, oc, od, p, wb, wo, fg, final):
    n = x.shape[0]
    tm = min(n, 256)

    def row(w):
        return pl.BlockSpec((tm, w), lambda i: (i, 0))

    def const(a):
        return pl.BlockSpec(a.shape, lambda i: (0,) * a.ndim)

    return pl.pallas_call(
        functools.partial(_merge_kernel, final=final),
        grid=(n // tm,),
        in_specs=[row(D_MODEL), row(BRANCH_W), row(BRANCH_W), row(BRANCH_W), row(BRANCH_W),
                  pl.BlockSpec((tm, N_BRANCH * D_MODEL), lambda i: (i, OFF_MERGE)),
                  const(wb), const(wo), const(fg)],
        out_specs=row(D_MODEL),
        out_shape=jax.ShapeDtypeStruct((n, D_MODEL), f32),
        compiler_params=_cparams("parallel"),
        name="merge",
    )(x, oa, ob, oc, od, p, wb, wo, fg)


def _pad_rows(w, height):
    return jnp.pad(w, ((0, height - w.shape[0]), (0, 0)))


def _layout_w_in(w_t):
    def seg(i):
        return w_t[_IN_OFFS[i]:_IN_OFFS[i + 1]]

    dqi = seg(_DQI)
    dqi_pad = jnp.concatenate(
        [_pad_rows(dqi[h * DSA_IDX_DIM:(h + 1) * DSA_IDX_DIM], LANE) for h in range(DSA_IDX_HEADS)], axis=0)
    parts = [seg(_MERGE), seg(_GV), seg(_GGATE), seg(_LX), seg(_LGATE), seg(_CGATE), seg(_DQ), seg(_DGATE),
             dqi_pad, seg(_GQ), seg(_GK), seg(_CQ), seg(_CKV), seg(_DK), seg(_DV),
             _pad_rows(seg(_GLR), LANE), _pad_rows(seg(_KR), LANE),
             _pad_rows(jnp.concatenate([seg(_DKI), seg(_DWI)], axis=0), LANE)]
    return _pad_rows(jnp.concatenate(parts, axis=0), W_PAD).astype(MXU_DTYPE)


def _rope_tables(pos):
    half = MLA_ROPE // 2
    inv = ROPE_THETA ** (-jnp.arange(half, dtype=f32) / half)
    ang = pos.astype(f32)[:, None] * inv[None, :]
    lane = np.arange(LANE)
    sign = np.where((lane % MLA_ROPE) < half, -1.0, 1.0).astype(np.float32)
    return jnp.tile(jnp.cos(ang), (1, LANE // half)), jnp.tile(jnp.sin(ang), (1, LANE // half)) * sign


def _block_diag(w):
    eye = jnp.eye(LRU_BLOCKS, dtype=w.dtype)
    return jnp.einsum('nde,nm->ndme', w, eye).reshape(LRU_W, LRU_W).astype(MXU_DTYPE)


def kernel(x_prompt, x_sample, cache_mla, cache_dsa_kv, cache_dsa_kidx, state_gla, state_lru_h, state_lru_conv, page_table, ln_gain, w_in, gla_w_g2, gla_b_g, gla_norm, lru_conv_w, lru_conv_b, lru_w_r, lru_b_r, lru_w_i, lru_b_i, lru_lambda, mla_q_norm, mla_w_uq, mla_kv_norm, mla_w_uk, mla_w_uv, w_branch, w_out, final_gain):
    B, S, D = x_prompt.shape
    DB, T, _ = x_sample.shape
    depth = w_in.shape[0]
    n_pages = page_table.shape[1]
    past_len = n_pages * PAGE_SIZE
    topk_p = min(DSA_TOPK, S // 4)
    topk_s = min(DSA_TOPK, (past_len + T) // 4)
    assert D == D_MODEL and T <= SUB and S % 256 == 0 and S >= 512

    cos_p, sin_p = _rope_tables(jnp.arange(S))
    cos_s, sin_s = _rope_tables(past_len + jnp.arange(SUB))
    seqs_per_tile = min(DB * SUB, 256) // SUB
    cos_s = jnp.tile(cos_s, (seqs_per_tile, 1))
    sin_s = jnp.tile(sin_s, (seqs_per_tile, 1))
    w_in_t = jnp.transpose(w_in, (2, 0, 1))
    cache_mla_t = jnp.swapaxes(cache_mla, 2, 3)
    cache_ki_t = jnp.swapaxes(cache_dsa_kidx, 2, 3)
    cache_kv = cache_dsa_kv.reshape(cache_dsa_kv.shape[:2] + (2 * PAGE_SIZE, DSA_HD))

    xp = x_prompt.reshape(B * S, D)
    xs = jnp.pad(x_sample, ((0, 0), (0, SUB - T), (0, 0))).reshape(DB * SUB, D)
    fg = final_gain.reshape(1, D)
    outs = [[] for _ in range(12)]
    for l in range(depth):
        final = l == depth - 1
        w_l = _layout_w_in(w_in_t[:, l, :])
        gain = ln_gain[l].reshape(1, D)
        wg2 = jnp.pad(gla_w_g2[l], ((0, LANE - GLA_GATE_RANK), (0, 0))).astype(MXU_DTYPE)
        bg = gla_b_g[l].reshape(1, -1)
        gn = gla_norm[l].reshape(1, -1)
        cw = lru_conv_w[l]
        cb = lru_conv_b[l].reshape(1, -1)
        wr = _block_diag(lru_w_r[l])
        wi = _block_diag(lru_w_i[l])
        br = lru_b_r[l].reshape(1, -1)
        bi = lru_b_i[l].reshape(1, -1)
        lam = lru_lambda[l].reshape(1, -1)
        qn = mla_q_norm[l].reshape(1, -1)
        kvn = mla_kv_norm[l].reshape(1, -1)
        wuq3 = mla_w_uq[l].reshape(MLA_Q_LORA, MLA_HEADS, MLA_NOPE + MLA_ROPE)
        wuq = jnp.concatenate([wuq3[:, :, :MLA_NOPE].reshape(MLA_Q_LORA, -1),
                               wuq3[:, :, MLA_NOPE:].reshape(MLA_Q_LORA, -1)], axis=1).astype(MXU_DTYPE)
        wuk = mla_w_uk[l].reshape(MLA_KV_LORA, MLA_HEADS * MLA_NOPE).T.astype(MXU_DTYPE)
        wuv = jnp.transpose(mla_w_uv[l], (1, 0, 2)).astype(MXU_DTYPE)
        wb = w_branch[l].astype(MXU_DTYPE)
        wo = w_out[l].astype(MXU_DTYPE)

        p = _inproj(xp, gain, w_l)
        rows, mk, mq, kvf, kvb, kif, kib, dqb, dqib = _prep(p, cos_p, sin_p, qn, wuq, kvn, wuk)
        o_a, gla_s = _gla_prompt(p, B, S, wg2, bg, gn)
        o_b, lru_h, conv_new = _lru_prompt(p, B, S, cw, cb, wr, br, wi, bi, lam)
        o_c = _mla_prompt(mq, mk, p, B, S, wuv)
        o_d = _dsa_prompt(dqb, dqib, kib, kvb, p, B, S, topk_p)
        xp = _merge(xp, o_a, o_b, o_c, o_d, p, wb, wo, fg, final)
        outs[0].append(rows.reshape(B, S, MLA_ROW))
        outs[2].append(kvf.reshape(B, S, 2, DSA_KV_HEADS, DSA_HD))
        outs[4].append(kif.reshape(B, S, DSA_IDX_DIM))
        outs[6].append(gla_s)
        outs[8].append(lru_h.reshape(B, LRU_W))
        outs[10].append(conv_new)

        p = _inproj(xs, gain, w_l)
        rows, mk, mq, kvf, kvb, kif, kib, dqb, dqib = _prep(p, cos_s, sin_s, qn, wuq, kvn, wuk)
        p2 = p.reshape(DB, SUB * W_PAD)
        p3 = p.reshape(DB, SUB, W_PAD)
        o_a, gla_s = _gla_sample(p2, state_gla[l], T, wg2, bg, gn)
        o_b, lru_h, conv_new = _lru_sample(p2, state_lru_conv[l].reshape(DB, -1), state_lru_h[l], T,
                                           cw, cb, wr, br, wi, bi, lam)
        o_c = _mla_sample(page_table, cache_mla_t, l, mq.reshape(DB, SUB, -1), mk.reshape(DB, SUB, -1), p3, wuv)
        o_d = _dsa_sample(page_table, cache_ki_t, cache_kv, l, dqb.reshape(DB, SUB, -1),
                          dqib.reshape(DB, SUB, -1), p3, topk_s)
        xs = _merge(xs, o_a.reshape(DB * SUB, -1), o_b.reshape(DB * SUB, -1), o_c.reshape(DB * SUB, -1),
                    o_d.reshape(DB * SUB, -1), p, wb, wo, fg, final)
        outs[1].append(rows.reshape(DB, SUB, MLA_ROW)[:, :T])
        outs[3].append(kvf.reshape(DB, SUB, 2, DSA_KV_HEADS, DSA_HD)[:, :T])
        outs[5].append(kif.reshape(DB, SUB, DSA_IDX_DIM)[:, :T])
        outs[7].append(gla_s)
        outs[9].append(lru_h)
        outs[11].append(conv_new.reshape(DB, CONV_W - 1, LRU_W))

    y_prompt = xp.reshape(B, S, D)
    y_sample = xs.reshape(DB, SUB, D)[:, :T]
    return (y_prompt, y_sample) + tuple(jnp.stack(o) for o in outs)
```

```python
import functools
import math

import numpy as np
import jax
import jax.numpy as jnp
from jax import lax
from jax.experimental import pallas as pl
from jax.experimental.pallas import tpu as pltpu

f32 = jnp.float32
bf16 = jnp.bfloat16
MXU_DTYPE = bf16

LANE = 128
SUB = 8
VMEM_LIMIT = 56 * 1024 * 1024
PAGES_PER_STEP = 64

D_MODEL = 1024
PAGE_SIZE = 128
N_BRANCH = 4
BRANCH_W = D_MODEL // 2
GLA_HEADS = 4
GLA_DV = BRANCH_W // GLA_HEADS
GLA_DK = GLA_DV // 2
GLA_GATE_RANK = 16
GLA_TAU = 16.0
GLA_CHUNK = 64
LRU_W = BRANCH_W
LRU_BLOCKS = 8
LRU_BW = LRU_W // LRU_BLOCKS
LRU_C = 8.0
CONV_W = 4
MLA_HEADS = 4
MLA_NOPE = 64
MLA_ROPE = 32
MLA_V = BRANCH_W // MLA_HEADS
MLA_Q_LORA = D_MODEL // 4
MLA_KV_LORA = D_MODEL // 4
MLA_ROW = MLA_KV_LORA + MLA_ROPE
MLA_SCALE = (MLA_NOPE + MLA_ROPE) ** -0.5
MLA_KPAD = MLA_KV_LORA + LANE
ROPE_THETA = 10000.0
DSA_HEADS = 4
DSA_KV_HEADS = 1
DSA_HD = BRANCH_W // DSA_HEADS
DSA_IDX_HEADS = 4
DSA_IDX_DIM = 64
DSA_TOPK = 256
EPS = 1e-6
IN_SIZES = (
    GLA_HEADS * GLA_DK, GLA_HEADS * GLA_DK, GLA_HEADS * GLA_DV, GLA_GATE_RANK, BRANCH_W,
    LRU_W, LRU_W,
    MLA_Q_LORA, MLA_KV_LORA, MLA_ROPE, BRANCH_W,
    DSA_HEADS * DSA_HD, DSA_KV_HEADS * DSA_HD, DSA_KV_HEADS * DSA_HD,
    DSA_IDX_HEADS * DSA_IDX_DIM, DSA_IDX_DIM, DSA_IDX_HEADS, BRANCH_W,
    N_BRANCH * D_MODEL,
)
(_GQ, _GK, _GV, _GLR, _GGATE, _LX, _LGATE, _CQ, _CKV, _KR, _CGATE,
 _DQ, _DK, _DV, _DQI, _DKI, _DWI, _DGATE, _MERGE) = range(len(IN_SIZES))
_IN_OFFS = np.concatenate([[0], np.cumsum(IN_SIZES)]).tolist()

OFF_MERGE = 0
OFF_GV = 4096
OFF_GGATE = 4608
OFF_LX = 5120
OFF_LGATE = 5632
OFF_CGATE = 6144
OFF_DQ = 6656
OFF_DGATE = 7168
OFF_DQI = 7680
OFF_GQ = 8192
OFF_GK = 8448
OFF_CQ = 8704
OFF_CKV = 8960
OFF_DKV = 9216
OFF_GLR = 9472
OFF_KR = 9600
OFF_DKI = 9728
W_PAD = 10240

KEY_NEG_INF = -2139095041
INT_MIN = -2147483648

NT_DIMS = (((1,), (1,)), ((), ()))
TN_DIMS = (((0,), (0,)), ((), ()))


def _cparams(*sem):
    return pltpu.CompilerParams(dimension_semantics=sem, vmem_limit_bytes=VMEM_LIMIT)


def _mm(a, b):
    return jnp.dot(a.astype(MXU_DTYPE), b.astype(MXU_DTYPE), preferred_element_type=f32)


def _mm_nt(a, b):
    return lax.dot_general(a.astype(MXU_DTYPE), b.astype(MXU_DTYPE), NT_DIMS, preferred_element_type=f32)


def _mm_tn(a, b):
    return lax.dot_general(a.astype(MXU_DTYPE), b.astype(MXU_DTYPE), TN_DIMS, preferred_element_type=f32)


def _rms(x, g):
    return x * lax.rsqrt(jnp.mean(x * x, axis=-1, keepdims=True) + EPS) * g


def _silu(x):
    return x * jax.nn.sigmoid(x)


def _softplus(x):
    return jnp.maximum(x, 0.0) + jnp.log1p(jnp.exp(-jnp.abs(x)))


def _neg_expm1(y):
    acc = jnp.full_like(y, 1.0 / 479001600.0)
    for k in (39916800.0, 3628800.0, 362880.0, 40320.0, 5040.0, 720.0, 120.0, 24.0, 6.0, 2.0, 1.0):
        acc = acc * y + 1.0 / k
    return jnp.where(y > -0.25, -(acc * y), 1.0 - jnp.exp(y))


def _sort_key(x):
    b = lax.bitcast_convert_type(x, jnp.int32)
    return b ^ ((b >> 31) & jnp.int32(0x7FFFFFFF))


def _softmax_update(s, v, m_ref, l_ref, acc_ref, v_transposed=False):
    m_old = m_ref[...]
    m_new = jnp.maximum(m_old, jnp.max(s, axis=1, keepdims=True))
    m_safe = jnp.where(m_new == -jnp.inf, 0.0, m_new)
    alpha = jnp.exp(m_old - m_safe)
    p = jnp.exp(s - m_safe)
    l_ref[...] = alpha * l_ref[...] + jnp.sum(p, axis=1, keepdims=True)
    acc_ref[...] = alpha * acc_ref[...] + (_mm_nt(p, v) if v_transposed else _mm(p, v))
    m_ref[...] = m_new


def _softmax_init(m_ref, l_ref, acc_ref):
    m_ref[...] = jnp.full(m_ref.shape, -jnp.inf, f32)
    l_ref[...] = jnp.zeros(l_ref.shape, f32)
    acc_ref[...] = jnp.zeros(acc_ref.shape, f32)


def _stack_heads(x, n, w):
    return jnp.concatenate([x[:, h * w:(h + 1) * w] for h in range(n)], axis=0)


def _inproj_kernel(x_ref, g_ref, w_ref, o_ref, h_ref):
    @pl.when(pl.program_id(1) == 0)
    def _():
        h_ref[...] = _rms(x_ref[...], g_ref[...]).astype(h_ref.dtype)

    o_ref[...] = lax.dot_general(h_ref[...], w_ref[...], NT_DIMS, preferred_element_type=f32)


def _inproj(x, gain, w_t):
    n, d = x.shape
    tm = min(n, 1024)
    tn = 512
    return pl.pallas_call(
        _inproj_kernel,
        grid=(n // tm, W_PAD // tn),
        in_specs=[pl.BlockSpec((tm, d), lambda i, j: (i, 0)),
                  pl.BlockSpec((1, d), lambda i, j: (0, 0)),
                  pl.BlockSpec((tn, d), lambda i, j: (j, 0))],
        out_specs=pl.BlockSpec((tm, tn), lambda i, j: (i, j)),
        out_shape=jax.ShapeDtypeStruct((n, W_PAD), f32),
        scratch_shapes=[pltpu.VMEM((tm, d), MXU_DTYPE)],
        compiler_params=_cparams("parallel", "arbitrary"),
        name="inproj",
    )(x, gain, w_t)


def _prep_kernel(cq_ref, ckv_ref, kr_ref, dq_ref, dqi_ref, dkv_ref, dki_ref, cos_ref, sin_ref,
                 qn_ref, wuq_ref, kvn_ref, wuk_ref,
                 rows_ref, mk_ref, mq_ref, kvf_ref, kvb_ref, kif_ref, kib_ref, dqb_ref, dqib_ref):
    tm = cq_ref.shape[0]
    cos = cos_ref[...]
    sin = sin_ref[...]
    lane = lax.broadcasted_iota(jnp.int32, (tm, LANE), 1)
    first_half = (lane % MLA_ROPE) < (MLA_ROPE // 2)

    def rope(x):
        swapped = jnp.where(first_half, pltpu.roll(x, LANE - MLA_ROPE // 2, 1), pltpu.roll(x, MLA_ROPE // 2, 1))
        return x * cos + swapped * sin

    c = _rms(ckv_ref[...], kvn_ref[...])
    kr = rope(kr_ref[...])
    rows_ref[:, 0:MLA_KV_LORA] = c
    rows_ref[:, MLA_KV_LORA:MLA_ROW] = kr[:, 0:MLA_ROPE]
    mk_ref[:, 0:MLA_KV_LORA] = c.astype(mk_ref.dtype)
    mk_ref[:, MLA_KV_LORA:MLA_KPAD] = kr.astype(mk_ref.dtype)

    q = _mm(_rms(cq_ref[...], qn_ref[...]), wuq_ref[...])
    q_nope = q[:, 0:MLA_HEADS * MLA_NOPE]
    q_rope = rope(q[:, MLA_HEADS * MLA_NOPE:])
    lane_q = lax.broadcasted_iota(jnp.int32, q_nope.shape, 1)
    wuk = wuk_ref[...]
    for h in range(MLA_HEADS):
        q_h = jnp.where(lane_q // MLA_NOPE == h, q_nope, 0.0)
        q_lat = _mm(q_h, wuk)
        base = h * MLA_KPAD
        mq_ref[:, base:base + MLA_KV_LORA] = (q_lat * MLA_SCALE).astype(mq_ref.dtype)
        qr = q_rope if h == 0 else pltpu.roll(q_rope, LANE - h * MLA_ROPE, 1)
        qr = jnp.where(lane < MLA_ROPE, qr, 0.0)
        mq_ref[:, base + MLA_KV_LORA:base + MLA_KPAD] = (qr * MLA_SCALE).astype(mq_ref.dtype)

    dkv = dkv_ref[...]
    kvf_ref[...] = dkv
    kvb_ref[...] = dkv.astype(kvb_ref.dtype)
    seg = dki_ref[...]
    kif_ref[...] = seg[:, 0:DSA_IDX_DIM]
    kib_ref[...] = jnp.where(lane < DSA_IDX_DIM, seg, 0.0).astype(kib_ref.dtype)
    dqb_ref[...] = (dq_ref[...] * (DSA_HD ** -0.5)).astype(dqb_ref.dtype)
    dqib_ref[...] = (dqi_ref[...] * (DSA_IDX_DIM ** -0.5)).astype(dqib_ref.dtype)


def _prep(p, cos_t, sin_t, q_norm, w_uq, kv_norm, w_uk):
    n = p.shape[0]
    tm = min(n, 256)
    nt = cos_t.shape[0] // tm

    def seg(off, w):
        return pl.BlockSpec((tm, w), lambda i: (i, off // w))

    def const(a):
        return pl.BlockSpec(a.shape, lambda i: (0,) * a.ndim)

    def out(w):
        return pl.BlockSpec((tm, w), lambda i: (i, 0))

    outs = [(MLA_ROW, f32), (MLA_KPAD, MXU_DTYPE), (MLA_HEADS * MLA_KPAD, MXU_DTYPE),
            (2 * DSA_HD, f32), (2 * DSA_HD, MXU_DTYPE), (DSA_IDX_DIM, f32), (LANE, MXU_DTYPE),
            (DSA_HEADS * DSA_HD, MXU_DTYPE), (DSA_IDX_HEADS * LANE, MXU_DTYPE)]
    return pl.pallas_call(
        _prep_kernel,
        grid=(n // tm,),
        in_specs=[seg(OFF_CQ, 256), seg(OFF_CKV, 256), seg(OFF_KR, LANE), seg(OFF_DQ, 512),
                  seg(OFF_DQI, 512), seg(OFF_DKV, 256), seg(OFF_DKI, LANE),
                  pl.BlockSpec((tm, LANE), lambda i: (i % nt, 0)),
                  pl.BlockSpec((tm, LANE), lambda i: (i % nt, 0)),
                  const(q_norm), const(w_uq), const(kv_norm), const(w_uk)],
        out_specs=[out(w) for w, _ in outs],
        out_shape=[jax.ShapeDtypeStruct((n, w), dt) for w, dt in outs],
        compiler_params=_cparams("parallel"),
        name="prep",
    )(p, p, p, p, p, p, p, cos_t, sin_t, q_norm, w_uq, kv_norm, w_uk)


def _gla_gate(glr, wg2, bg):
    z = _mm(glr, wg2) + bg
    return (jnp.minimum(z, 0.0) - jnp.log1p(jnp.exp(-jnp.abs(z)))) * (1.0 / GLA_TAU)


def _gla_out(o, gate, gn):
    return _rms(o, gn) * _silu(gate)


def _gla_prompt_kernel(q_ref, k_ref, v_ref, gate_ref, glr_ref, wg2_ref, bg_ref, gn_ref,
                       o_ref, s_ref, st_ref, *, n_chunks):
    t = pl.program_id(1)

    @pl.when(t == 0)
    def _():
        st_ref[...] = jnp.zeros(st_ref.shape, f32)

    C = GLA_CHUNK
    HK = GLA_HEADS * GLA_DK
    r_i = lax.broadcasted_iota(jnp.int32, (C, C), 0)
    c_i = lax.broadcasted_iota(jnp.int32, (C, C), 1)
    causal = r_i >= c_i
    tril = causal.astype(f32)
    lane_head = lax.broadcasted_iota(jnp.int32, (C, HK), 1) // GLA_DK
    lane_head_st = lax.broadcasted_iota(jnp.int32, (GLA_DV, HK), 1) // GLA_DK
    causal_h = jnp.concatenate([causal] * GLA_HEADS, axis=0)
    gn = gn_ref[...]
    wg2 = wg2_ref[...]
    bg = bg_ref[...]
    for c in range(n_chunks):
        rows = slice(c * C, (c + 1) * C)
        g = _gla_gate(glr_ref[rows, :], wg2, bg)
        b = jnp.dot(tril, g, precision=lax.Precision.HIGHEST, preferred_element_type=f32)
        b_last = b[C - 1:C, :]
        q = q_ref[rows, :] * (GLA_DK ** -0.5)
        k = k_ref[rows, :]
        v = v_ref[rows, :]
        q_dec = q * jnp.exp(b)
        k_dec = k * jnp.exp(-b)
        k_tail = k * jnp.exp(b_last - b)
        decay = jnp.exp(b_last)
        q_heads = jnp.concatenate(
            [jnp.where(lane_head == h, q_dec, 0.0) for h in range(GLA_HEADS)], axis=0)
        att = jnp.where(causal_h, _mm_nt(q_heads, k_dec), 0.0)
        st = st_ref[...]
        o_inter = _mm_nt(q_heads, st)
        gate = gate_ref[rows, :]
        for h in range(GLA_HEADS):
            hv = slice(h * GLA_DV, (h + 1) * GLA_DV)
            o_h = _mm(att[h * C:(h + 1) * C, :], v[:, hv]) + o_inter[h * C:(h + 1) * C, :]
            o_ref[rows, hv] = _gla_out(o_h, gate[:, hv], gn)
        kv = _mm_tn(v, k_tail)
        upd = jnp.zeros((GLA_DV, HK), f32)
        for h in range(GLA_HEADS):
            upd = upd + jnp.where(lane_head_st == h, kv[h * GLA_DV:(h + 1) * GLA_DV, :], 0.0)
        st_ref[...] = decay * st + upd

    @pl.when(t == pl.num_programs(1) - 1)
    def _():
        s_ref[...] = st_ref[...].T.reshape(GLA_HEADS, GLA_DK, GLA_DV)


def _gla_prompt(p, B, L, wg2, bg, gn):
    tl = min(L, 256)
    nt = L // tl

    def seg(off, w):
        return pl.BlockSpec((tl, w), lambda b, t: (b * nt + t, off // w))

    def const(a):
        return pl.BlockSpec(a.shape, lambda b, t: (0,) * a.ndim)

    return pl.pallas_call(
        functools.partial(_gla_prompt_kernel, n_chunks=tl // GLA_CHUNK),
        grid=(B, nt),
        in_specs=[seg(OFF_GQ, 256), seg(OFF_GK, 256), seg(OFF_GV, 512), seg(OFF_GGATE, 512),
                  seg(OFF_GLR, LANE), const(wg2), const(bg), const(gn)],
        out_specs=[pl.BlockSpec((tl, BRANCH_W), lambda b, t: (b * nt + t, 0)),
                   pl.BlockSpec((None, GLA_HEADS, GLA_DK, GLA_DV), lambda b, t: (b, 0, 0, 0))],
        out_shape=[jax.ShapeDtypeStruct((B * L, BRANCH_W), f32),
                   jax.ShapeDtypeStruct((B, GLA_HEADS, GLA_DK, GLA_DV), f32)],
        scratch_shapes=[pltpu.VMEM((GLA_DV, GLA_HEADS * GLA_DK), f32)],
        compiler_params=_cparams("parallel", "arbitrary"),
        name="gla_prompt",
    )(p, p, p, p, p, wg2, bg, gn)


def _gla_sample_kernel(*refs, n_tok, bg_rows):
    q_refs = refs[0:n_tok]
    k_refs = refs[n_tok:2 * n_tok]
    v_refs = refs[2 * n_tok:3 * n_tok]
    gate_refs = refs[3 * n_tok:4 * n_tok]
    glr_refs = refs[4 * n_tok:5 * n_tok]
    s0_ref, wg2_ref, bg_ref, gn_ref, o_ref, s_ref, tr_ref, oraw_ref = refs[5 * n_tok:]
    HK = GLA_HEADS * GLA_DK
    zpad = jnp.zeros((LANE - bg_rows, HK), f32)

    def col_major(x):
        return jnp.concatenate([x, zpad], axis=0).T

    for t in range(n_tok):
        g = _gla_gate(glr_refs[t][...], wg2_ref[...], bg_ref[...])
        tr_ref[3 * t + 0] = col_major(q_refs[t][...] * (GLA_DK ** -0.5))
        tr_ref[3 * t + 1] = col_major(k_refs[t][...])
        tr_ref[3 * t + 2] = col_major(jnp.exp(g))
    for i in range(bg_rows):
        s = s0_ref[i].reshape(HK, GLA_DV)
        for t in range(n_tok):
            qc = tr_ref[3 * t + 0, :, i:i + 1]
            kc = tr_ref[3 * t + 1, :, i:i + 1]
            ec = tr_ref[3 * t + 2, :, i:i + 1]
            vrow = v_refs[t][i:i + 1, :]
            vb = jnp.concatenate(
                [jnp.broadcast_to(vrow[:, h * GLA_DV:(h + 1) * GLA_DV], (GLA_DK, GLA_DV))
                 for h in range(GLA_HEADS)], axis=0)
            s = ec * s + kc * vb
            prod = qc * s
            oraw_ref[t, i:i + 1, :] = jnp.concatenate(
                [jnp.sum(prod[h * GLA_DK:(h + 1) * GLA_DK, :], axis=0, keepdims=True)
                 for h in range(GLA_HEADS)], axis=1)
        s_ref[i] = s.reshape(GLA_HEADS, GLA_DK, GLA_DV)
    gn = gn_ref[...]
    for t in range(n_tok):
        o = oraw_ref[t]
        gate = gate_refs[t][...]
        for h in range(GLA_HEADS):
            hv = slice(h * GLA_DV, (h + 1) * GLA_DV)
            o_ref[:, t * BRANCH_W + h * GLA_DV:t * BRANCH_W + (h + 1) * GLA_DV] = _gla_out(o[:, hv], gate[:, hv], gn)
    o_ref[:, n_tok * BRANCH_W:] = jnp.zeros((bg_rows, (SUB - n_tok) * BRANCH_W), f32)


def _gla_sample(p2, s0, n_tok, wg2, bg, gn):
    db = p2.shape[0]
    bgr = min(db, 16)

    def seg(t, off, w):
        return pl.BlockSpec((bgr, w), lambda i: (i, (t * W_PAD + off) // w))

    def const(a):
        return pl.BlockSpec(a.shape, lambda i: (0,) * a.ndim)

    in_specs = ([seg(t, OFF_GQ, 256) for t in range(n_tok)] + [seg(t, OFF_GK, 256) for t in range(n_tok)]
                + [seg(t, OFF_GV, 512) for t in range(n_tok)] + [seg(t, OFF_GGATE, 512) for t in range(n_tok)]
                + [seg(t, OFF_GLR, LANE) for t in range(n_tok)]
                + [pl.BlockSpec((bgr, GLA_HEADS, GLA_DK, GLA_DV), lambda i: (i, 0, 0, 0)),
                   const(wg2), const(bg), const(gn)])
    return pl.pallas_call(
        functools.partial(_gla_sample_kernel, n_tok=n_tok, bg_rows=bgr),
        grid=(db // bgr,),
        in_specs=in_specs,
        out_specs=[pl.BlockSpec((bgr, SUB * BRANCH_W), lambda i: (i, 0)),
                   pl.BlockSpec((bgr, GLA_HEADS, GLA_DK, GLA_DV), lambda i: (i, 0, 0, 0))],
        out_shape=[jax.ShapeDtypeStruct((db, SUB * BRANCH_W), f32),
                   jax.ShapeDtypeStruct(s0.shape, f32)],
        scratch_shapes=[pltpu.VMEM((3 * n_tok, GLA_HEADS * GLA_DK, LANE), f32),
                        pltpu.VMEM((n_tok, bgr, BRANCH_W), f32)],
        compiler_params=_cparams("parallel"),
        name="gla_sample",
    )(*([p2] * (5 * n_tok)), s0, wg2, bg, gn)


def _lru_gates(xc, wr, br, wi, bi, sp):
    r = jax.nn.sigmoid(_mm(xc, wr) + br)
    ig = jax.nn.sigmoid(_mm(xc, wi) + bi)
    log_a = -LRU_C * r * sp
    return jnp.exp(log_a), jnp.sqrt(_neg_expm1(2.0 * log_a)) * (ig * xc)


def _lru_prompt_kernel(x_ref, gate_ref, cw_ref, cb_ref, wr_ref, br_ref, wi_ref, bi_ref, lam_ref,
                       o_ref, hfin_ref, conv_ref, tail_ref, h_ref, a_ref, u_ref, hs_ref):
    t = pl.program_id(1)
    tl = x_ref.shape[0]

    @pl.when(t == 0)
    def _():
        tail_ref[...] = jnp.zeros(tail_ref.shape, f32)
        h_ref[...] = jnp.zeros(h_ref.shape, f32)

    x = x_ref[...]
    ext = jnp.concatenate([tail_ref[...], x], axis=0)
    cw = cw_ref[...]
    xc = jnp.zeros(x.shape, f32)
    for j in range(CONV_W):
        s = SUB - (CONV_W - 1) + j
        xc = xc + ext[s:s + tl, :] * cw[j:j + 1, :]
    xc = cb_ref[...] + xc
    a, u = _lru_gates(xc, wr_ref[...], br_ref[...], wi_ref[...], bi_ref[...], _softplus(-lam_ref[...]))
    a_ref[...] = a
    u_ref[...] = u

    def step(i, h):
        h = a_ref[pl.ds(i, 1), :] * h + u_ref[pl.ds(i, 1), :]
        hs_ref[pl.ds(i, 1), :] = h
        return h

    h = lax.fori_loop(0, tl, step, h_ref[...], unroll=8)
    h_ref[...] = h
    tail_ref[...] = x[tl - SUB:, :]
    o_ref[...] = hs_ref[...] * _silu(gate_ref[...])

    @pl.when(t == pl.num_programs(1) - 1)
    def _():
        hfin_ref[...] = h
        conv_ref[...] = x[tl - (CONV_W - 1):, :]


def _lru_prompt(p, B, L, cw, cb, wr, br, wi, bi, lam):
    tl = min(L, 256)
    nt = L // tl

    def seg(off, w):
        return pl.BlockSpec((tl, w), lambda b, t: (b * nt + t, off // w))

    def const(a):
        return pl.BlockSpec(a.shape, lambda b, t: (0,) * a.ndim)

    return pl.pallas_call(
        _lru_prompt_kernel,
        grid=(B, nt),
        in_specs=[seg(OFF_LX, 512), seg(OFF_LGATE, 512), const(cw), const(cb), const(wr), const(br),
                  const(wi), const(bi), const(lam)],
        out_specs=[pl.BlockSpec((tl, LRU_W), lambda b, t: (b * nt + t, 0)),
                   pl.BlockSpec((None, 1, LRU_W), lambda b, t: (b, 0, 0)),
                   pl.BlockSpec((None, CONV_W - 1, LRU_W), lambda b, t: (b, 0, 0))],
        out_shape=[jax.ShapeDtypeStruct((B * L, LRU_W), f32),
                   jax.ShapeDtypeStruct((B, 1, LRU_W), f32),
                   jax.ShapeDtypeStruct((B, CONV_W - 1, LRU_W), f32)],
        scratch_shapes=[pltpu.VMEM((SUB, LRU_W), f32), pltpu.VMEM((1, LRU_W), f32),
                        pltpu.VMEM((tl, LRU_W), f32), pltpu.VMEM((tl, LRU_W), f32),
                        pltpu.VMEM((tl, LRU_W), f32)],
        compiler_params=_cparams("parallel", "arbitrary"),
        name="lru_prompt",
    )(p, p, cw, cb, wr, br, wi, bi, lam)


def _lru_sample_kernel(*refs, n_tok):
    x_refs = refs[0:n_tok]
    gate_refs = refs[n_tok:2 * n_tok]
    (buf_ref, h0_ref, cw_ref, cb_ref, wr_ref, br_ref, wi_ref, bi_ref, lam_ref,
     o_ref, hfin_ref, conv_ref) = refs[2 * n_tok:]
    W = LRU_W
    xs = [buf_ref[:, j * W:(j + 1) * W] for j in range(CONV_W - 1)] + [r[...] for r in x_refs]
    cw = cw_ref[...]
    sp = _softplus(-lam_ref[...])
    h = h0_ref[...]
    for t in range(n_tok):
        xc = jnp.zeros(h.shape, f32)
        for j in range(CONV_W):
            xc = xc + xs[t + j] * cw[j:j + 1, :]
        xc = cb_ref[...] + xc
        a, u = _lru_gates(xc, wr_ref[...], br_ref[...], wi_ref[...], bi_ref[...], sp)
        h = a * h + u
        o_ref[:, t * W:(t + 1) * W] = h * _silu(gate_refs[t][...])
    o_ref[:, n_tok * W:] = jnp.zeros((h.shape[0], (SUB - n_tok) * W), f32)
    hfin_ref[...] = h
    for j in range(CONV_W - 1):
        conv_ref[:, j * W:(j + 1) * W] = xs[n_tok + j]


def _lru_sample(p2, buf, h0, n_tok, cw, cb, wr, br, wi, bi, lam):
    db = p2.shape[0]

    def seg(t, off, w):
        return pl.BlockSpec((db, w), lambda i: (0, (t * W_PAD + off) // w))

    def const(a):
        return pl.BlockSpec(a.shape, lambda i: (0,) * a.ndim)

    in_specs = ([seg(t, OFF_LX, 512) for t in range(n_tok)] + [seg(t, OFF_LGATE, 512) for t in range(n_tok)]
                + [const(a) for a in (buf, h0, cw, cb, wr, br, wi, bi, lam)])
    return pl.pallas_call(
        functools.partial(_lru_sample_kernel, n_tok=n_tok),
        grid=(1,),
        in_specs=in_specs,
        out_specs=[pl.BlockSpec((db, SUB * LRU_W), lambda i: (0, 0)),
                   pl.BlockSpec((db, LRU_W), lambda i: (0, 0)),
                   pl.BlockSpec((db, (CONV_W - 1) * LRU_W), lambda i: (0, 0))],
        out_shape=[jax.ShapeDtypeStruct((db, SUB * LRU_W), f32),
                   jax.ShapeDtypeStruct((db, LRU_W), f32),
                   jax.ShapeDtypeStruct((db, (CONV_W - 1) * LRU_W), f32)],
        compiler_params=_cparams("arbitrary"),
        name="lru_sample",
    )(*([p2] * (2 * n_tok)), buf, h0, cw, cb, wr, br, wi, bi, lam)


def _mla_finish(acc_ref, l_ref, wuv_ref, gate, o_ref, rows):
    o = acc_ref[...] / l_ref[...]
    for h in range(MLA_HEADS):
        oc = _mm(o[h * rows:(h + 1) * rows, :], wuv_ref[h])
        o_ref[:, h * MLA_V:(h + 1) * MLA_V] = oc * _silu(gate[:, h * MLA_V:(h + 1) * MLA_V])


def _mla_prompt_kernel(q_ref, k_ref, wuv_ref, gate_ref, o_ref, m_ref, l_ref, acc_ref, *, tq, tk):
    i = pl.program_id(1)
    qs = _stack_heads(q_ref[...], MLA_HEADS, MLA_KPAD)
    _softmax_init(m_ref, l_ref, acc_ref)
    n_full = (i * tq) // tk

    def keys(j):
        return k_ref[pl.ds(pl.multiple_of(j * tk, tk), tk), :]

    def body(j, s):
        s_next = _mm_nt(qs, keys(j + 1))
        _softmax_update(s, keys(j)[:, 0:MLA_KV_LORA], m_ref, l_ref, acc_ref)
        return s_next

    s = lax.fori_loop(0, n_full, body, _mm_nt(qs, keys(0)))
    row = i * tq + lax.broadcasted_iota(jnp.int32, s.shape, 0) % tq
    col = n_full * tk + lax.broadcasted_iota(jnp.int32, s.shape, 1)
    _softmax_update(jnp.where(col <= row, s, -jnp.inf), keys(n_full)[:, 0:MLA_KV_LORA], m_ref, l_ref, acc_ref)
    _mla_finish(acc_ref, l_ref, wuv_ref, gate_ref[...], o_ref, tq)


def _mla_prompt(mq, mk, p, B, L, wuv):
    tq = 128
    tk = min(L, 512)
    nq = L // tq
    rows = MLA_HEADS * tq
    return pl.pallas_call(
        functools.partial(_mla_prompt_kernel, tq=tq, tk=tk),
        grid=(B, nq),
        in_specs=[pl.BlockSpec((tq, MLA_HEADS * MLA_KPAD), lambda b, i: (b * nq + i, 0)),
                  pl.BlockSpec((L, MLA_KPAD), lambda b, i: (b, 0)),
                  pl.BlockSpec(wuv.shape, lambda b, i: (0, 0, 0)),
                  pl.BlockSpec((tq, BRANCH_W), lambda b, i: (b * nq + i, OFF_CGATE // BRANCH_W))],
        out_specs=pl.BlockSpec((tq, BRANCH_W), lambda b, i: (b * nq + i, 0)),
        out_shape=jax.ShapeDtypeStruct((B * L, BRANCH_W), f32),
        scratch_shapes=[pltpu.VMEM((rows, 1), f32), pltpu.VMEM((rows, 1), f32),
                        pltpu.VMEM((rows, MLA_KV_LORA), f32)],
        compiler_params=_cparams("parallel", "arbitrary"),
        name="mla_prompt",
    )(mq, mk, wuv, p)


def _new_block(x):
    return jnp.concatenate([x.astype(f32), jnp.zeros((LANE - SUB, x.shape[1]), f32)], axis=0)


def _mla_sample_kernel(pt_ref, q_ref, knew_ref, wuv_ref, gate_ref, *rest, n_grp):
    page_refs = rest[0:n_grp]
    o_ref, m_ref, l_ref, acc_ref = rest[n_grp:]
    j = pl.program_id(1)

    @pl.when(j == 0)
    def _():
        _softmax_init(m_ref, l_ref, acc_ref)

    qs = _stack_heads(q_ref[...], MLA_HEADS, MLA_KPAD)
    k_t = jnp.concatenate([r[...].astype(MXU_DTYPE) for r in page_refs], axis=1)
    k_t_pad = jnp.concatenate([k_t, jnp.zeros((MLA_KPAD - MLA_ROW, k_t.shape[1]), MXU_DTYPE)], axis=0)
    _softmax_update(_mm(qs, k_t_pad), k_t[0:MLA_KV_LORA, :], m_ref, l_ref, acc_ref, v_transposed=True)

    @pl.when(j == pl.num_programs(1) - 1)
    def _():
        kn = _new_block(knew_ref[...]).astype(MXU_DTYPE)
        s = _mm_nt(qs, kn)
        tok = lax.broadcasted_iota(jnp.int32, s.shape, 0) % SUB
        col = lax.broadcasted_iota(jnp.int32, s.shape, 1)
        _softmax_update(jnp.where(col <= tok, s, -jnp.inf), kn[:, 0:MLA_KV_LORA], m_ref, l_ref, acc_ref)
        _mla_finish(acc_ref, l_ref, wuv_ref, gate_ref[...], o_ref, SUB)


def _mla_sample(page_table, cache, layer, mq3, mk3, p3, wuv):
    db, n_pages = page_table.shape
    n_grp = math.gcd(n_pages, PAGES_PER_STEP)
    n_steps = n_pages // n_grp
    rows = MLA_HEADS * SUB

    def page_spec(g):
        return pl.BlockSpec((None, None, MLA_ROW, PAGE_SIZE),
                            lambda b, j, pt: (layer, pt[b * n_pages + j * n_grp + g], 0, 0))

    grid_spec = pltpu.PrefetchScalarGridSpec(
        num_scalar_prefetch=1,
        grid=(db, n_steps),
        in_specs=[pl.BlockSpec((None, SUB, MLA_HEADS * MLA_KPAD), lambda b, j, pt: (b, 0, 0)),
                  pl.BlockSpec((None, SUB, MLA_KPAD), lambda b, j, pt: (b, 0, 0)),
                  pl.BlockSpec(wuv.shape, lambda b, j, pt: (0, 0, 0)),
                  pl.BlockSpec((None, SUB, BRANCH_W), lambda b, j, pt: (b, 0, OFF_CGATE // BRANCH_W))]
                 + [page_spec(g) for g in range(n_grp)],
        out_specs=pl.BlockSpec((None, SUB, BRANCH_W), lambda b, j, pt: (b, 0, 0)),
        scratch_shapes=[pltpu.VMEM((rows, 1), f32), pltpu.VMEM((rows, 1), f32),
                        pltpu.VMEM((rows, MLA_KV_LORA), f32)])
    return pl.pallas_call(
        functools.partial(_mla_sample_kernel, n_grp=n_grp),
        grid_spec=grid_spec,
        out_shape=jax.ShapeDtypeStruct((db, SUB, BRANCH_W), f32),
        compiler_params=_cparams("parallel", "arbitrary"),
        name="mla_sample",
    )(page_table.reshape(-1), mq3, mk3, wuv, p3, *([cache] * n_grp))


def _index_scores(qi_stack, wcols, keys_b, rows, keys_transposed=False):
    return _weighted_relu(_mm(qi_stack, keys_b) if keys_transposed else _mm_nt(qi_stack, keys_b), wcols, rows)


def _weighted_relu(raw, wcols, rows):
    sc = jnp.maximum(raw, 0.0)
    tot = sc[0:rows, :] * wcols[0]
    for h in range(1, DSA_IDX_HEADS):
        tot = tot + sc[h * rows:(h + 1) * rows, :] * wcols[h]
    return jnp.where(tot == 0.0, 0.0, tot)


def _wi_cols(seg):
    scale = DSA_IDX_HEADS ** -0.5
    return [seg[:, DSA_IDX_DIM + h:DSA_IDX_DIM + h + 1] * scale for h in range(DSA_IDX_HEADS)]


def _kth_largest(count_ge, rows, topk, count_ge_hi=None):
    hi = count_ge_hi is not None
    first = count_ge_hi if hi else count_ge
    t0 = jnp.where(first(jnp.zeros((rows, 1), jnp.int32)) >= topk,
                   jnp.int32(0), jnp.int32(-(1 << 15) if hi else INT_MIN))

    def bits(count_fn, top_bit, n_bits, t):
        def bit_body(it, t):
            cand = t + lax.shift_left(jnp.int32(1), jnp.int32(top_bit) - it)
            return jnp.where(count_fn(cand) >= topk, cand, t)
        return lax.fori_loop(0, n_bits, bit_body, t)

    if not hi:
        return bits(count_ge, 30, 31, t0)
    t_hi = bits(count_ge_hi, 14, 15, t0)
    return bits(count_ge, 15, 16, t_hi * jnp.int32(1 << 16))


def _dsa_finish(acc_ref, l_ref, gate, o_ref, rows):
    o = acc_ref[...] / l_ref[...]
    for h in range(DSA_HEADS):
        hv = slice(h * DSA_HD, (h + 1) * DSA_HD)
        o_ref[:, hv] = o[h * rows:(h + 1) * rows, :] * _silu(gate[:, hv])


def _dsa_prompt_kernel(q_ref, qi_ref, wi_ref, gate_ref, kib_ref, kvb_ref, u_ref, o_ref,
                       keys_ref, khi_ref, m_ref, l_ref, acc_ref, *, tq, tk, topk):
    i = pl.program_id(1)
    n_ch = (i * tq) // tk + 1
    qi = _stack_heads(qi_ref[...], DSA_IDX_HEADS, LANE)
    wcols = _wi_cols(wi_ref[...])
    row_pos = i * tq + lax.broadcasted_iota(jnp.int32, (tq, tk), 0)
    col_in = lax.broadcasted_iota(jnp.int32, (tq, tk), 1)

    def score_body(j, carry):
        kb = kib_ref[pl.ds(pl.multiple_of(j * tk, tk), tk), :]
        tot = _index_scores(qi, wcols, kb, tq)
        kk = _sort_key(jnp.where(j * tk + col_in <= row_pos, tot, -jnp.inf))
        keys_ref[j] = kk
        khi_ref[j] = (kk >> 16).astype(jnp.int16)
        return carry

    lax.fori_loop(0, n_ch, score_body, 0)

    def count(pred, bound):
        bound_b = jnp.broadcast_to(bound, (tq, LANE))

        def body(j, acc):
            for c in range(tk // LANE):
                acc = acc + jnp.where(pred(keys_ref[j, :, c * LANE:(c + 1) * LANE], bound_b), 1.0, 0.0)
            return acc

        acc = lax.fori_loop(0, n_ch, body, jnp.zeros((tq, LANE), f32))
        return jnp.sum(acc, axis=1, keepdims=True)

    def count_hi(bound):
        bound_b = jnp.broadcast_to(bound.astype(jnp.int16), (tq, LANE))
        one = jnp.ones((tq, LANE), jnp.int16)
        zero = jnp.zeros((tq, LANE), jnp.int16)

        def body(j, acc):
            for c in range(tk // LANE):
                acc = acc + jnp.where(khi_ref[j, :, c * LANE:(c + 1) * LANE] >= bound_b, one, zero)
            return acc

        acc = lax.fori_loop(0, n_ch, body, zero)
        return jnp.sum(acc.astype(f32), axis=1, keepdims=True)

    thr = _kth_largest(lambda cand: count(lambda kk, bb: kk >= bb, cand), tq, topk, count_ge_hi=count_hi)
    need = topk - count(lambda kk, bb: kk > bb, thr)
    tie_ok = thr > KEY_NEG_INF
    qs = _stack_heads(q_ref[...], DSA_HEADS, DSA_HD)
    _softmax_init(m_ref, l_ref, acc_ref)

    def kv_block(j):
        return kvb_ref[pl.ds(pl.multiple_of(j * tk, tk), tk), :]

    def qk(j):
        return _mm_nt(qs, kv_block(j)[:, 0:DSA_HD])

    def consume(j, s, run):
        kk = keys_ref[j]
        eq = (kk == thr) & tie_ok
        eqf = eq.astype(f32)
        before = _mm(eqf, u_ref[...]) + run
        sel = (kk > thr) | (eq & (before < need))
        s = jnp.where(jnp.concatenate([sel] * DSA_HEADS, axis=0), s, -jnp.inf)
        _softmax_update(s, kv_block(j)[:, DSA_HD:], m_ref, l_ref, acc_ref)
        return run + jnp.sum(eqf, axis=1, keepdims=True)

    def att_body(j, carry):
        s, run = carry
        s_next = qk(j + 1)
        return s_next, consume(j, s, run)

    s, run = lax.fori_loop(0, n_ch - 1, att_body, (qk(0), jnp.zeros((tq, 1), f32)))
    consume(n_ch - 1, s, run)
    _dsa_finish(acc_ref, l_ref, gate_ref[...], o_ref, tq)


def _dsa_prompt(dqb, dqib, kib, kvb, p, B, L, topk):
    tq = 128
    tk = min(L, 512)
    nq = L // tq
    rows = DSA_HEADS * tq
    upper = (np.arange(tk)[:, None] < np.arange(tk)[None, :]).astype(np.float32)
    u = jnp.asarray(upper, MXU_DTYPE)
    return pl.pallas_call(
        functools.partial(_dsa_prompt_kernel, tq=tq, tk=tk, topk=topk),
        grid=(B, nq),
        in_specs=[pl.BlockSpec((tq, DSA_HEADS * DSA_HD), lambda b, i: (b * nq + i, 0)),
                  pl.BlockSpec((tq, DSA_IDX_HEADS * LANE), lambda b, i: (b * nq + i, 0)),
                  pl.BlockSpec((tq, LANE), lambda b, i: (b * nq + i, OFF_DKI // LANE)),
                  pl.BlockSpec((tq, BRANCH_W), lambda b, i: (b * nq + i, OFF_DGATE // BRANCH_W)),
                  pl.BlockSpec((L, LANE), lambda b, i: (b, 0)),
                  pl.BlockSpec((L, 2 * DSA_HD), lambda b, i: (b, 0)),
                  pl.BlockSpec((tk, tk), lambda b, i: (0, 0))],
        out_specs=pl.BlockSpec((tq, BRANCH_W), lambda b, i: (b * nq + i, 0)),
        out_shape=jax.ShapeDtypeStruct((B * L, BRANCH_W), f32),
        scratch_shapes=[pltpu.VMEM((L // tk, tq, tk), jnp.int32), pltpu.VMEM((L // tk, tq, tk), jnp.int16),
                        pltpu.VMEM((rows, 1), f32), pltpu.VMEM((rows, 1), f32),
                        pltpu.VMEM((rows, DSA_HD), f32)],
        compiler_params=_cparams("parallel", "arbitrary"),
        name="dsa_prompt",
    )(dqb, dqib, p, p, kib, kvb, u)


def _lane_prefix(x):
    n = x.shape[1]
    lane = lax.broadcasted_iota(jnp.int32, x.shape, 1)
    s = 1
    while s < n:
        x = x + jnp.where(lane >= s, pltpu.roll(x, s, 1), 0.0)
        s *= 2
    return x


def _dsa_scores_kernel(pt_ref, qi_ref, seg_ref, *rest, n_grp):
    ki_refs = rest[0:n_grp]
    keys_ref, knew_ref = rest[n_grp:]
    j = pl.program_id(1)
    seg = seg_ref[...]
    wcols = _wi_cols(seg)
    qi = _stack_heads(qi_ref[...], DSA_IDX_HEADS, LANE)
    k_t = jnp.concatenate([r[...].astype(MXU_DTYPE) for r in ki_refs], axis=1)
    k_t = jnp.concatenate([k_t, jnp.zeros((LANE - DSA_IDX_DIM, k_t.shape[1]), MXU_DTYPE)], axis=0)
    keys = _sort_key(_index_scores(qi, wcols, k_t, SUB, keys_transposed=True))
    for g in range(n_grp):
        keys_ref[g] = keys[:, g * PAGE_SIZE:(g + 1) * PAGE_SIZE]

    @pl.when(j == pl.num_programs(1) - 1)
    def _():
        lane = lax.broadcasted_iota(jnp.int32, (LANE, LANE), 1)
        kn = jnp.where(lane < DSA_IDX_DIM, _new_block(seg), 0.0)
        tot = _index_scores(qi, wcols, kn, SUB)
        tok = lax.broadcasted_iota(jnp.int32, tot.shape, 0)
        col = lax.broadcasted_iota(jnp.int32, tot.shape, 1)
        knew_ref[...] = _sort_key(jnp.where(col <= tok, tot, -jnp.inf))


def _dsa_threshold_kernel(keys_ref, knew_ref, thr_ref, need_ref, ties_ref, *, topk):
    n_tiles, rb, _ = keys_ref.shape

    def count(pred, bound):
        bound_b = jnp.broadcast_to(bound, (rb, LANE))

        def body(c, acc):
            return acc + jnp.where(pred(keys_ref[c], bound_b), 1.0, 0.0)

        acc = lax.fori_loop(0, n_tiles, body, jnp.where(pred(knew_ref[...], bound_b), 1.0, 0.0),
                            unroll=math.gcd(n_tiles, 8))
        return jnp.sum(acc, axis=1, keepdims=True)

    thr = _kth_largest(lambda cand: count(lambda kk, bb: kk >= bb, cand), rb, topk)
    need = topk - count(lambda kk, bb: kk > bb, thr)
    n_eq = count(lambda kk, bb: kk == bb, thr)
    thr_ref[...] = thr
    need_ref[...] = need
    ties_ref[...] = jnp.where((thr > KEY_NEG_INF) & (n_eq > need), 1.0, 0.0)


def _dsa_attend_kernel(pt_ref, q_ref, kvnew_ref, gate_ref, keys_ref, knew_ref, thr_ref, need_ref, ties_ref,
                       *rest, n_grp):
    kv_refs = rest[0:n_grp]
    o_ref, run_ref, m_ref, l_ref, acc_ref = rest[n_grp:]
    j = pl.program_id(1)

    @pl.when(j == 0)
    def _():
        run_ref[...] = jnp.zeros(run_ref.shape, f32)
        _softmax_init(m_ref, l_ref, acc_ref)

    ties_matter = jnp.max(ties_ref[...]) > 0.5

    def select(kk):
        thr = thr_ref[...]
        eq = (kk == thr) & (thr > KEY_NEG_INF)

        def by_position():
            eqf = eq.astype(f32)
            incl = _lane_prefix(eqf)
            picked = eq & (run_ref[...] + incl - eqf < need_ref[...])
            run_ref[...] = run_ref[...] + incl[:, -1:]
            return jnp.where((kk > thr) | picked, 1.0, 0.0)

        sel = lax.cond(ties_matter, by_position, lambda: jnp.where((kk > thr) | eq, 1.0, 0.0))
        return jnp.concatenate([sel] * DSA_HEADS, axis=0) > 0.5

    qs = _stack_heads(q_ref[...], DSA_HEADS, DSA_HD)
    sel = select(jnp.concatenate([keys_ref[g] for g in range(n_grp)], axis=1))
    k = jnp.concatenate([r[pl.ds(0, PAGE_SIZE, stride=2), :].astype(MXU_DTYPE) for r in kv_refs], axis=0)
    v = jnp.concatenate([r[pl.ds(1, PAGE_SIZE, stride=2), :].astype(MXU_DTYPE) for r in kv_refs], axis=0)
    _softmax_update(jnp.where(sel, _mm_nt(qs, k), -jnp.inf), v, m_ref, l_ref, acc_ref)

    @pl.when(j == pl.num_programs(1) - 1)
    def _():
        sel_n = select(knew_ref[...])
        kvn = _new_block(kvnew_ref[...]).astype(MXU_DTYPE)
        s_n = jnp.where(sel_n, _mm_nt(qs, kvn[:, 0:DSA_HD]), -jnp.inf)
        _softmax_update(s_n, kvn[:, DSA_HD:], m_ref, l_ref, acc_ref)
        _dsa_finish(acc_ref, l_ref, gate_ref[...], o_ref, SUB)


def _dsa_sample(page_table, cache_ki, cache_kv, layer, dqb3, dqib3, p3, topk):
    db, n_pages = page_table.shape
    n_grp = math.gcd(n_pages, PAGES_PER_STEP)
    n_steps = n_pages // n_grp
    n_rows = db * SUB
    rows = DSA_HEADS * SUB
    pt = page_table.reshape(-1)

    def page_spec(shape):
        def spec(g):
            return pl.BlockSpec((None, None) + shape,
                                lambda b, j, pt: (layer, pt[b * n_pages + j * n_grp + g], 0, 0))
        return [spec(g) for g in range(n_grp)]

    def row3(w, col):
        return pl.BlockSpec((None, SUB, w), lambda b, j, pt: (b, 0, col))

    keys_spec = pl.BlockSpec((n_grp, SUB, PAGE_SIZE), lambda b, j, pt: (j, b, 0))
    per_seq = lambda w: pl.BlockSpec((SUB, w), lambda b, j, pt: (b, 0))

    keys, knew = pl.pallas_call(
        functools.partial(_dsa_scores_kernel, n_grp=n_grp),
        grid_spec=pltpu.PrefetchScalarGridSpec(
            num_scalar_prefetch=1,
            grid=(db, n_steps),
            in_specs=[row3(DSA_IDX_HEADS * LANE, 0), row3(LANE, OFF_DKI // LANE)]
                     + page_spec((DSA_IDX_DIM, PAGE_SIZE)),
            out_specs=[keys_spec, per_seq(LANE)]),
        out_shape=[jax.ShapeDtypeStruct((n_pages, n_rows, PAGE_SIZE), jnp.int32),
                   jax.ShapeDtypeStruct((n_rows, LANE), jnp.int32)],
        compiler_params=_cparams("parallel", "arbitrary"),
        name="dsa_scores",
    )(pt, dqib3, p3, *([cache_ki] * n_grp))

    rb = min(n_rows, LANE)
    thr, need, ties = pl.pallas_call(
        functools.partial(_dsa_threshold_kernel, topk=topk),
        grid=(n_rows // rb,),
        in_specs=[pl.BlockSpec((n_pages, rb, PAGE_SIZE), lambda i: (0, i, 0)),
                  pl.BlockSpec((rb, LANE), lambda i: (i, 0))],
        out_specs=[pl.BlockSpec((rb, 1), lambda i: (i, 0))] * 3,
        out_shape=[jax.ShapeDtypeStruct((n_rows, 1), jnp.int32), jax.ShapeDtypeStruct((n_rows, 1), f32),
                   jax.ShapeDtypeStruct((n_rows, 1), f32)],
        compiler_params=_cparams("parallel"),
        name="dsa_threshold",
    )(keys, knew)

    return pl.pallas_call(
        functools.partial(_dsa_attend_kernel, n_grp=n_grp),
        grid_spec=pltpu.PrefetchScalarGridSpec(
            num_scalar_prefetch=1,
            grid=(db, n_steps),
            in_specs=[row3(DSA_HEADS * DSA_HD, 0), row3(2 * DSA_HD, OFF_DKV // (2 * DSA_HD)),
                      row3(BRANCH_W, OFF_DGATE // BRANCH_W), keys_spec, per_seq(LANE), per_seq(1), per_seq(1), per_seq(1)]
                     + page_spec((2 * PAGE_SIZE, DSA_HD)),
            out_specs=pl.BlockSpec((None, SUB, BRANCH_W), lambda b, j, pt: (b, 0, 0)),
            scratch_shapes=[pltpu.VMEM((SUB, 1), f32),
                            pltpu.VMEM((rows, 1), f32), pltpu.VMEM((rows, 1), f32),
                            pltpu.VMEM((rows, DSA_HD), f32)]),
        out_shape=jax.ShapeDtypeStruct((db, SUB, BRANCH_W), f32),
        compiler_params=_cparams("parallel", "arbitrary"),
        name="dsa_attend",
    )(pt, dqb3, p3, p3, keys, knew, thr, need, ties, *([cache_kv] * n_grp))


def _merge_kernel(x_ref, oa_ref, ob_ref, oc_ref, od_ref, mg_ref, wb_ref, wo_ref, fg_ref, y_ref, *, final):
    mixed = jnp.zeros(x_ref.shape, f32)
    for n, o_ref in enumerate((oa_ref, ob_ref, oc_ref, od_ref)):
        gate = jax.nn.sigmoid(mg_ref[:, n * D_MODEL:(n + 1) * D_MODEL])
        mixed = mixed + gate * _mm(o_ref[...], wb_ref[n])
    y = x_ref[...] + _mm(mixed, wo_ref[...])
    y_ref[...] = _rms(y, fg_ref[...]) if final else y


def _merge(x, oa, ob, oc, od, p, wb, wo, fg, final):
    n = x.shape[0]
    tm = min(n, 256)

    def row(w):
        return pl.BlockSpec((tm, w), lambda i: (i, 0))

    def const(a):
        return pl.BlockSpec(a.shape, lambda i: (0,) * a.ndim)

    return pl.pallas_call(
        functools.partial(_merge_kernel, final=final),
        grid=(n // tm,),
        in_specs=[row(D_MODEL), row(BRANCH_W), row(BRANCH_W), row(BRANCH_W), row(BRANCH_W),
                  pl.BlockSpec((tm, N_BRANCH * D_MODEL), lambda i: (i, OFF_MERGE)),
                  const(wb), const(wo), const(fg)],
        out_specs=row(D_MODEL),
        out_shape=jax.ShapeDtypeStruct((n, D_MODEL), f32),
        compiler_params=_cparams("parallel"),
        name="merge",
    )(x, oa, ob, oc, od, p, wb, wo, fg)


def _pad_rows(w, height):
    return jnp.pad(w, ((0, height - w.shape[0]), (0, 0)))


def _layout_w_in(w_t):
    def seg(i):
        return w_t[_IN_OFFS[i]:_IN_OFFS[i + 1]]

    dqi = seg(_DQI)
    dqi_pad = jnp.concatenate(
        [_pad_rows(dqi[h * DSA_IDX_DIM:(h + 1) * DSA_IDX_DIM], LANE) for h in range(DSA_IDX_HEADS)], axis=0)
    parts = [seg(_MERGE), seg(_GV), seg(_GGATE), seg(_LX), seg(_LGATE), seg(_CGATE), seg(_DQ), seg(_DGATE),
             dqi_pad, seg(_GQ), seg(_GK), seg(_CQ), seg(_CKV), seg(_DK), seg(_DV),
             _pad_rows(seg(_GLR), LANE), _pad_rows(seg(_KR), LANE),
             _pad_rows(jnp.concatenate([seg(_DKI), seg(_DWI)], axis=0), LANE)]
    return _pad_rows(jnp.concatenate(parts, axis=0), W_PAD).astype(MXU_DTYPE)


def _rope_tables(pos):
    half = MLA_ROPE // 2
    inv = ROPE_THETA ** (-jnp.arange(half, dtype=f32) / half)
    ang = pos.astype(f32)[:, None] * inv[None, :]
    lane = np.arange(LANE)
    sign = np.where((lane % MLA_ROPE) < half, -1.0, 1.0).astype(np.float32)
    return jnp.tile(jnp.cos(ang), (1, LANE // half)), jnp.tile(jnp.sin(ang), (1, LANE // half)) * sign


def _block_diag(w):
    eye = jnp.eye(LRU_BLOCKS, dtype=w.dtype)
    return jnp.einsum('nde,nm->ndme', w, eye).reshape(LRU_W, LRU_W).astype(MXU_DTYPE)


def kernel(x_prompt, x_sample, cache_mla, cache_dsa_kv, cache_dsa_kidx, state_gla, state_lru_h, state_lru_conv, page_table, ln_gain, w_in, gla_w_g2, gla_b_g, gla_norm, lru_conv_w, lru_conv_b, lru_w_r, lru_b_r, lru_w_i, lru_b_i, lru_lambda, mla_q_norm, mla_w_uq, mla_kv_norm, mla_w_uk, mla_w_uv, w_branch, w_out, final_gain):
    B, S, D = x_prompt.shape
    DB, T, _ = x_sample.shape
    depth = w_in.shape[0]
    n_pages = page_table.shape[1]
    past_len = n_pages * PAGE_SIZE
    topk_p = min(DSA_TOPK, S // 4)
    topk_s = min(DSA_TOPK, (past_len + T) // 4)
    assert D == D_MODEL and T <= SUB and S % 256 == 0 and S >= 512

    cos_p, sin_p = _rope_tables(jnp.arange(S))
    cos_s, sin_s = _rope_tables(past_len + jnp.arange(SUB))
    seqs_per_tile = min(DB * SUB, 256) // SUB
    cos_s = jnp.tile(cos_s, (seqs_per_tile, 1))
    sin_s = jnp.tile(sin_s, (seqs_per_tile, 1))
    w_in_t = jnp.transpose(w_in, (2, 0, 1))
    cache_mla_t = jnp.swapaxes(cache_mla, 2, 3)
    cache_ki_t = jnp.swapaxes(cache_dsa_kidx, 2, 3)
    cache_kv = cache_dsa_kv.reshape(cache_dsa_kv.shape[:2] + (2 * PAGE_SIZE, DSA_HD))

    xp = x_prompt.reshape(B * S, D)
    xs = jnp.pad(x_sample, ((0, 0), (0, SUB - T), (0, 0))).reshape(DB * SUB, D)
    fg = final_gain.reshape(1, D)
    outs = [[] for _ in range(12)]
    for l in range(depth):
        final = l == depth - 1
        w_l = _layout_w_in(w_in_t[:, l, :])
        gain = ln_gain[l].reshape(1, D)
        wg2 = jnp.pad(gla_w_g2[l], ((0, LANE - GLA_GATE_RANK), (0, 0))).astype(MXU_DTYPE)
        bg = gla_b_g[l].reshape(1, -1)
        gn = gla_norm[l].reshape(1, -1)
        cw = lru_conv_w[l]
        cb = lru_conv_b[l].reshape(1, -1)
        wr = _block_diag(lru_w_r[l])
        wi = _block_diag(lru_w_i[l])
        br = lru_b_r[l].reshape(1, -1)
        bi = lru_b_i[l].reshape(1, -1)
        lam = lru_lambda[l].reshape(1, -1)
        qn = mla_q_norm[l].reshape(1, -1)
        kvn = mla_kv_norm[l].reshape(1, -1)
        wuq3 = mla_w_uq[l].reshape(MLA_Q_LORA, MLA_HEADS, MLA_NOPE + MLA_ROPE)
        wuq = jnp.concatenate([wuq3[:, :, :MLA_NOPE].reshape(MLA_Q_LORA, -1),
                               wuq3[:, :, MLA_NOPE:].reshape(MLA_Q_LORA, -1)], axis=1).astype(MXU_DTYPE)
        wuk = mla_w_uk[l].reshape(MLA_KV_LORA, MLA_HEADS * MLA_NOPE).T.astype(MXU_DTYPE)
        wuv = jnp.transpose(mla_w_uv[l], (1, 0, 2)).astype(MXU_DTYPE)
        wb = w_branch[l].astype(MXU_DTYPE)
        wo = w_out[l].astype(MXU_DTYPE)

        p = _inproj(xp, gain, w_l)
        rows, mk, mq, kvf, kvb, kif, kib, dqb, dqib = _prep(p, cos_p, sin_p, qn, wuq, kvn, wuk)
        o_a, gla_s = _gla_prompt(p, B, S, wg2, bg, gn)
        o_b, lru_h, conv_new = _lru_prompt(p, B, S, cw, cb, wr, br, wi, bi, lam)
        o_c = _mla_prompt(mq, mk, p, B, S, wuv)
        o_d = _dsa_prompt(dqb, dqib, kib, kvb, p, B, S, topk_p)
        xp = _merge(xp, o_a, o_b, o_c, o_d, p, wb, wo, fg, final)
        outs[0].append(rows.reshape(B, S, MLA_ROW))
        outs[2].append(kvf.reshape(B, S, 2, DSA_KV_HEADS, DSA_HD))
        outs[4].append(kif.reshape(B, S, DSA_IDX_DIM))
        outs[6].append(gla_s)
        outs[8].append(lru_h.reshape(B, LRU_W))
        outs[10].append(conv_new)

        p = _inproj(xs, gain, w_l)
        rows, mk, mq, kvf, kvb, kif, kib, dqb, dqib = _prep(p, cos_s, sin_s, qn, wuq, kvn, wuk)
        p2 = p.reshape(DB, SUB * W_PAD)
        p3 = p.reshape(DB, SUB, W_PAD)
        o_a, gla_s = _gla_sample(p2, state_gla[l], T, wg2, bg, gn)
        o_b, lru_h, conv_new = _lru_sample(p2, state_lru_conv[l].reshape(DB, -1), state_lru_h[l], T,
                                           cw, cb, wr, br, wi, bi, lam)
        o_c = _mla_sample(page_table, cache_mla_t, l, mq.reshape(DB, SUB, -1), mk.reshape(DB, SUB, -1), p3, wuv)
        o_d = _dsa_sample(page_table, cache_ki_t, cache_kv, l, dqb.reshape(DB, SUB, -1),
                          dqib.reshape(DB, SUB, -1), p3, topk_s)
        xs = _merge(xs, o_a.reshape(DB * SUB, -1), o_b.reshape(DB * SUB, -1), o_c.reshape(DB * SUB, -1),
                    o_d.reshape(DB * SUB, -1), p, wb, wo, fg, final)
        outs[1].append(rows.reshape(DB, SUB, MLA_ROW)[:, :T])
        outs[3].append(kvf.reshape(DB, SUB, 2, DSA_KV_HEADS, DSA_HD)[:, :T])
        outs[5].append(kif.reshape(DB, SUB, DSA_IDX_DIM)[:, :T])
        outs[7].append(gla_s)
        outs[9].append(lru_h)
        outs[11].append(conv_new.reshape(DB, CONV_W - 1, LRU_W))

    y_prompt = xp.reshape(B, S, D)
    y_sample = xs.reshape(DB, SUB, D)[:, :T]
    return (y_prompt, y_sample) + tuple(jnp.stack(o) for o in outs)
```
